```python
import math
import jax, jax.numpy as jnp
from jax import lax
import numpy as np

D_MODEL = 1024
BATCH = 2
SEQ = 8192
DEPTH = 1

D_MIX = D_MODEL
ATT_HEADS = 8
HEAD_DIM = 64
D_ATT = ATT_HEADS * HEAD_DIM
CONV_GROUPS = 8
D_CONV = D_MIX - D_ATT
CONV_WIDTH = 31
Q_BLOCK = 128
N_EXPERTS = 32
TOP_K = 4
D_FF = 1024
SWIGLU_LIMIT = 7.0
SWIGLU_ALPHA = 1.702
RMS_EPS = 1e-6
LN_EPS = 1e-5
MASK_VALUE = -1e30
D_IN = 3 * D_ATT + ATT_HEADS + 2 * D_CONV

kernel_name = "hybrid_fox_conformer_moe_layer"


def rmsnorm(x, g):
    xf = x.astype(jnp.float32)
    y = xf * lax.rsqrt(jnp.mean(xf * xf, axis=-1, keepdims=True) + RMS_EPS)
    return (y * g.astype(jnp.float32)).astype(x.dtype)


def layernorm(x, g, b):
    xf = x.astype(jnp.float32)
    mu = jnp.mean(xf, axis=-1, keepdims=True)
    var = jnp.mean(jnp.square(xf - mu), axis=-1, keepdims=True)
    y = (xf - mu) * lax.rsqrt(var + LN_EPS)
    return (y * g.astype(jnp.float32) + b.astype(jnp.float32)).astype(x.dtype)


def forgetting_attention(q, k, v, log_f):
    B, H, T, Dh = q.shape
    scale = 1.0 / math.sqrt(Dh)
    c = lax.cumsum(log_f, axis=2)
    k_pos = jnp.arange(T)
    n_blocks = T // Q_BLOCK

    def one_block(i):
        start = i * Q_BLOCK
        q_blk = lax.dynamic_slice_in_dim(q, start, Q_BLOCK, axis=2)
        c_blk = lax.dynamic_slice_in_dim(c, start, Q_BLOCK, axis=2)
        s = jnp.einsum('bhqd,bhkd->bhqk', q_blk, k).astype(jnp.float32) * scale
        s = s + c_blk[..., :, None] - c[..., None, :]
        q_pos = start + jnp.arange(Q_BLOCK)
        causal = k_pos[None, :] <= q_pos[:, None]
        s = jnp.where(causal, s, MASK_VALUE)
        p = jax.nn.softmax(s, axis=-1).astype(v.dtype)
        return jnp.einsum('bhqk,bhkd->bhqd', p, v)

    o = lax.map(one_block, jnp.arange(n_blocks))
    o = jnp.transpose(o, (1, 0, 3, 2, 4))
    return o.reshape(B, T, H * Dh)


def conformer_conv(glu_in, w_dw, b_dw, ln_g, ln_b):
    a, g = jnp.split(glu_in, 2, axis=-1)
    u = a * jax.nn.sigmoid(g)
    y = lax.conv_general_dilated(
        u, w_dw[:, None, :], window_strides=(1,),
        padding=[(CONV_WIDTH - 1, 0)],
        dimension_numbers=('NWC', 'WIO', 'NWC'),
        feature_group_count=D_CONV) + b_dw
    y = layernorm(y, ln_g, ln_b)
    return jax.nn.silu(y)


def moe_ffn(x, w_router, b_router, w_gu, b_gu, w_down, b_down):
    logits = (x @ w_router + b_router).astype(jnp.float32)
    top_vals, top_idx = lax.top_k(logits, TOP_K)
    top_w = jax.nn.softmax(top_vals, axis=-1)
    gates = jnp.sum(jax.nn.one_hot(top_idx, N_EXPERTS, dtype=jnp.float32)
                    * top_w[..., None], axis=1).astype(x.dtype)

    def expert(acc, params):
        wgu_e, bgu_e, wd_e, bd_e, gate_e = params
        h = x @ wgu_e + bgu_e
        hg, hu = h[:, :D_FF], h[:, D_FF:]
        hg = jnp.minimum(hg, SWIGLU_LIMIT)
        hu = jnp.clip(hu, -SWIGLU_LIMIT, SWIGLU_LIMIT)
        act = (hu + 1.0) * (hg * jax.nn.sigmoid(SWIGLU_ALPHA * hg))
        y = act @ wd_e + bd_e
        return acc + gate_e[:, None] * y, None

    acc0 = jnp.zeros_like(x)
    out, _ = lax.scan(expert, acc0, (w_gu, b_gu, w_down, b_down, gates.T))
    return out


def setup_inputs(seed: int = 0) -> dict:
    key = jax.random.key(seed)
    ks = jax.random.split(key, 20)
    f32 = jnp.float32
    nrm = lambda k, shape, s: jax.random.normal(k, shape, f32) * s
    return {
        "x": nrm(ks[0], (BATCH, SEQ, D_MODEL), 1.0),
        "norm_mix_g": 1.0 + nrm(ks[1], (D_MODEL,), 0.02),
        "w_in": nrm(ks[2], (D_MODEL, D_IN), D_MODEL ** -0.5),
        "b_f": jax.random.uniform(ks[3], (ATT_HEADS,), f32, 1.0, 5.0),
        "b_glu": nrm(ks[4], (2 * D_CONV,), 0.02),
        "w_dw": nrm(ks[5], (CONV_WIDTH, D_CONV), CONV_WIDTH ** -0.5),
        "b_dw": nrm(ks[6], (D_CONV,), 0.02),
        "ln_g": 1.0 + nrm(ks[7], (D_CONV,), 0.02),
        "ln_b": nrm(ks[8], (D_CONV,), 0.02),
        "w_out": nrm(ks[9], (D_MIX, D_MODEL), D_MIX ** -0.5),
        "norm_ffn_g": 1.0 + nrm(ks[10], (D_MODEL,), 0.02),
        "w_router": nrm(ks[11], (D_MODEL, N_EXPERTS), D_MODEL ** -0.5),
        "b_router": nrm(ks[12], (N_EXPERTS,), 0.01),
        "w_gu": nrm(ks[13], (N_EXPERTS, D_MODEL, 2 * D_FF), D_MODEL ** -0.5),
        "b_gu": nrm(ks[14], (N_EXPERTS, 2 * D_FF), 0.02),
        "w_down": nrm(ks[15], (N_EXPERTS, D_FF, D_MODEL), D_FF ** -0.5),
        "b_down": nrm(ks[16], (N_EXPERTS, D_MODEL), 0.02),
        "norm_final_g": 1.0 + nrm(ks[17], (D_MODEL,), 0.02),
    }


def reference(x, norm_mix_g, w_in, b_f, b_glu, w_dw, b_dw, ln_g, ln_b, w_out,
              norm_ffn_g, w_router, b_router, w_gu, b_gu, w_down, b_down,
              norm_final_g):
    B, T, D = x.shape
    h = x
    for _ in range(DEPTH):
        xn = rmsnorm(h, norm_mix_g)
        proj = xn @ w_in
        o0 = 0
        q = proj[..., o0:o0 + D_ATT]; o0 += D_ATT
        k = proj[..., o0:o0 + D_ATT]; o0 += D_ATT
        v = proj[..., o0:o0 + D_ATT]; o0 += D_ATT
        f_logit = proj[..., o0:o0 + ATT_HEADS]; o0 += ATT_HEADS
        glu_in = proj[..., o0:o0 + 2 * D_CONV] + b_glu

        heads = lambda t: jnp.transpose(t.reshape(B, T, ATT_HEADS, HEAD_DIM), (0, 2, 1, 3))
        log_f = jax.nn.log_sigmoid((f_logit + b_f).astype(jnp.float32))
        log_f = jnp.transpose(log_f, (0, 2, 1))
        att_o = forgetting_attention(heads(q), heads(k), heads(v), log_f)

        conv_o = conformer_conv(glu_in, w_dw, b_dw, ln_g, ln_b)

        mixed = jnp.concatenate([att_o, conv_o], axis=-1)
        h = h + mixed @ w_out

        hn = rmsnorm(h, norm_ffn_g).reshape(B * T, D)
        h = h + moe_ffn(hn, w_router, b_router, w_gu, b_gu, w_down, b_down).reshape(B, T, D)
    return rmsnorm(h, norm_final_g)
```

```python
import functools
import math

import numpy as np
import jax
import jax.numpy as jnp
from jax import lax
from jax.experimental import pallas as pl
from jax.experimental.pallas import tpu as pltpu

D_MODEL = 1024
ATT_HEADS = 8
HEAD_DIM = 64
D_ATT = ATT_HEADS * HEAD_DIM
D_CONV = D_MODEL - D_ATT
CONV_WIDTH = 31
N_EXPERTS = 32
TOP_K = 4
D_FF = 1024
SWIGLU_LIMIT = 7.0
SWIGLU_ALPHA = 1.702
RMS_EPS = 1e-6
LN_EPS = 1e-5
MASK_VALUE = -1e30

LANES = 128
ROW_TILE = 512
CONV_HALO = 32
GROUP_TILE = 256
GATHER_TILE = 256
VMEM_LIMIT = 56 * 1024 * 1024

AUG_Q = HEAD_DIM
AUG_K = HEAD_DIM + 3
ONES_LANE = ATT_HEADS

f32 = jnp.float32
bf16 = jnp.bfloat16


def _split3(x):
    hi = x.astype(bf16)
    r1 = x - hi.astype(f32)
    mid = r1.astype(bf16)
    lo = (r1 - mid.astype(f32)).astype(bf16)
    return hi, mid, lo


def _selection_matrices():
    sel_q = np.zeros((3, LANES, ATT_HEADS * LANES), np.float32)
    sel_k = np.zeros((3, ATT_HEADS * LANES, LANES), np.float32)
    for h in range(ATT_HEADS):
        base = h * LANES
        for p in range(3):
            sel_q[p, h, base + AUG_Q + p] = 1.0
            sel_q[0, ONES_LANE, base + AUG_K + p] = 1.0
            sel_k[p, base + AUG_K + p, h] = -1.0
            sel_k[0, base + AUG_Q + p, ONES_LANE] = 1.0
    return jnp.asarray(sel_q, bf16), jnp.asarray(sel_k, bf16)


def _inproj_kernel(x_ref, g_ref, wq_ref, wkt_ref, wv_ref, wf_ref, wglu_ref, bf_ref, bglu_ref,
                   selq_ref, selk_ref, tri_ref,
                   qa_ref, kta_ref, v_ref, u_ref, carry_ref, *, tiles_per_seq):
    i = pl.program_id(0)

    @pl.when(i % tiles_per_seq == 0)
    def _():
        carry_ref[...] = jnp.zeros_like(carry_ref)

    x = x_ref[...]
    xn = x * lax.rsqrt(jnp.mean(x * x, axis=-1, keepdims=True) + RMS_EPS) * g_ref[...]
    xb = xn.astype(bf16)

    f = jnp.dot(xb, wf_ref[...], preferred_element_type=f32) + bf_ref[...]
    log_f = jnp.minimum(f, 0.0) - jnp.log1p(jnp.exp(-jnp.abs(f)))
    lane = lax.broadcasted_iota(jnp.int32, log_f.shape, 1)
    log_f = jnp.where(lane < ATT_HEADS, log_f, 0.0)
    tri = tri_ref[...]
    hi, mid, lo = _split3(log_f)
    c = (jnp.dot(tri, hi, preferred_element_type=f32)
         + jnp.dot(tri, mid, preferred_element_type=f32)
         + jnp.dot(tri, lo, preferred_element_type=f32)) + carry_ref[...]
    tm = c.shape[0]
    carry_ref[...] = c[tm - 1:tm, :]

    c_hi, c_mid, c_lo = _split3(c)
    c_hi = jnp.where(lane == ONES_LANE, jnp.ones_like(c_hi), c_hi)
    qa = jnp.dot(xb, wq_ref[...], preferred_element_type=f32) * (1.0 / math.sqrt(HEAD_DIM))
    qa = (qa + jnp.dot(c_hi, selq_ref[0], preferred_element_type=f32)
          + jnp.dot(c_mid, selq_ref[1], preferred_element_type=f32)
          + jnp.dot(c_lo, selq_ref[2], preferred_element_type=f32)).astype(bf16)
    for h in range(ATT_HEADS):
        qa_ref[0, h] = qa[:, h * LANES:(h + 1) * LANES]

    ct = c.T
    ct_hi, ct_mid, ct_lo = _split3(ct)
    row = lax.broadcasted_iota(jnp.int32, ct.shape, 0)
    ct_hi = jnp.where(row == ONES_LANE, jnp.ones_like(ct_hi), ct_hi)
    kta = lax.dot_general(wkt_ref[...], xb, (((1,), (1,)), ((), ())), preferred_element_type=f32)
    kta = (kta + jnp.dot(selk_ref[0], ct_hi, preferred_element_type=f32)
           + jnp.dot(selk_ref[1], ct_mid, preferred_element_type=f32)
           + jnp.dot(selk_ref[2], ct_lo, preferred_element_type=f32)).astype(bf16)
    for h in range(ATT_HEADS):
        kta_ref[0, h, 0] = kta[h * LANES:(h + 1) * LANES, :]

    v_ref[...] = jnp.dot(xb, wv_ref[...], preferred_element_type=f32).astype(bf16)

    glu = jnp.dot(xb, wglu_ref[...], preferred_element_type=f32) + bglu_ref[...]
    a = glu[:, :D_CONV]
    gate = glu[:, D_CONV:]
    u_ref[...] = a * (1.0 / (1.0 + jnp.exp(-gate)))


def _inproj(x2d, norm_mix_g, w_in, b_f, b_glu, *, batch, seq):
    n = batch * seq
    tm = ROW_TILE
    nt = seq // tm
    wq = w_in[:, :D_ATT]
    wk = w_in[:, D_ATT:2 * D_ATT]
    wv = w_in[:, 2 * D_ATT:3 * D_ATT]
    wf = w_in[:, 3 * D_ATT:3 * D_ATT + ATT_HEADS]
    wglu = w_in[:, 3 * D_ATT + ATT_HEADS:]
    pad = LANES - HEAD_DIM
    wq_pad = jnp.pad(wq.reshape(D_MODEL, ATT_HEADS, HEAD_DIM), ((0, 0), (0, 0), (0, pad)))
    wq_pad = wq_pad.reshape(D_MODEL, ATT_HEADS * LANES).astype(bf16)
    wkt_pad = jnp.pad(wk.T.reshape(ATT_HEADS, HEAD_DIM, D_MODEL), ((0, 0), (0, pad), (0, 0)))
    wkt_pad = wkt_pad.reshape(ATT_HEADS * LANES, D_MODEL).astype(bf16)
    wf_pad = jnp.pad(wf, ((0, 0), (0, LANES - ATT_HEADS))).astype(bf16)
    bf_pad = jnp.pad(b_f, (0, LANES - ATT_HEADS)).reshape(1, LANES)
    sel_q, sel_k = _selection_matrices()
    tri = jnp.asarray(np.tril(np.ones((tm, tm), np.float32)), bf16)

    const = lambda shape: pl.BlockSpec(shape, lambda i: (0,) * len(shape))
    return pl.pallas_call(
        functools.partial(_inproj_kernel, tiles_per_seq=nt),
        grid=(n // tm,),
        in_specs=[
            pl.BlockSpec((tm, D_MODEL), lambda i: (i, 0)),
            const((1, D_MODEL)),
            const((D_MODEL, ATT_HEADS * LANES)),
            const((ATT_HEADS * LANES, D_MODEL)),
            const((D_MODEL, D_ATT)),
            const((D_MODEL, LANES)),
            const((D_MODEL, 2 * D_CONV)),
            const((1, LANES)),
            const((1, 2 * D_CONV)),
            const((3, LANES, ATT_HEADS * LANES)),
            const((3, ATT_HEADS * LANES, LANES)),
            const((tm, tm)),
        ],
        out_specs=[
            pl.BlockSpec((1, ATT_HEADS, tm, LANES), lambda i: (i // nt, 0, i % nt, 0)),
            pl.BlockSpec((1, ATT_HEADS, 1, LANES, tm), lambda i: (i // nt, 0, i % nt, 0, 0)),
            pl.BlockSpec((tm, D_ATT), lambda i: (i, 0)),
            pl.BlockSpec((tm, D_CONV), lambda i: (i, 0)),
        ],
        out_shape=[
            jax.ShapeDtypeStruct((batch, ATT_HEADS, seq, LANES), bf16),
            jax.ShapeDtypeStruct((batch, ATT_HEADS, nt, LANES, tm), bf16),
            jax.ShapeDtypeStruct((n, D_ATT), bf16),
            jax.ShapeDtypeStruct((n, D_CONV), f32),
        ],
        scratch_shapes=[pltpu.VMEM((1, LANES), f32)],
        compiler_params=pltpu.CompilerParams(
            dimension_semantics=("arbitrary",), vmem_limit_bytes=VMEM_LIMIT),
        name="inproj",
    )(x2d, norm_mix_g.reshape(1, D_MODEL), wq_pad, wkt_pad, wv.astype(bf16), wf_pad,
      wglu.astype(bf16), bf_pad, b_glu.reshape(1, 2 * D_CONV), sel_q, sel_k, tri)


def _attn_kernel(qa_ref, kta_ref, v_ref, o_ref, acc_ref, m_ref, *, tile):
    i = pl.program_id(2)
    lane = lax.broadcasted_iota(jnp.int32, (tile, LANES), 1)
    first_half = lane < HEAD_DIM
    acc_ref[...] = jnp.zeros_like(acc_ref)
    m_ref[...] = jnp.full_like(m_ref, MASK_VALUE)

    def step(j, masked):
        vp = v_ref[0, pl.ds(pl.multiple_of(j * tile, tile), tile), :]
        one = jnp.ones_like(vp)
        v_aug = (jnp.where(first_half, vp, one), jnp.where(first_half, one, vp))
        for h in range(2):
            s = jnp.dot(qa_ref[0, h], kta_ref[0, h, j], preferred_element_type=f32)
            if masked:
                r = lax.broadcasted_iota(jnp.int32, s.shape, 0)
                c = lax.broadcasted_iota(jnp.int32, s.shape, 1)
                s = jnp.where(c <= r, s, MASK_VALUE)
            m_old = m_ref[h]
            m_new = jnp.maximum(m_old, jnp.max(s, axis=-1, keepdims=True))
            alpha = jnp.exp(m_old - m_new)
            p = jnp.exp(s - m_new).astype(bf16)
            acc_ref[h] = alpha * acc_ref[h] + jnp.dot(p, v_aug[h], preferred_element_type=f32)
            m_ref[h] = m_new

    def body(j, carry):
        step(j, False)
        return carry

    lax.fori_loop(0, i, body, 0)
    step(i, True)

    a0 = acc_ref[0]
    a1 = acc_ref[1]
    l0 = a0[:, HEAD_DIM:HEAD_DIM + 1]
    l1 = a1[:, 0:1]
    o_ref[0] = jnp.where(first_half, a0 / l0, a1 / l1).astype(o_ref.dtype)


def _attention(qa, kta, v3, *, batch, seq):
    tile = ROW_TILE
    nt = seq // tile
    return pl.pallas_call(
        functools.partial(_attn_kernel, tile=tile),
        grid=(batch, ATT_HEADS // 2, nt),
        in_specs=[
            pl.BlockSpec((1, 2, tile, LANES), lambda b, hp, i: (b, hp, i, 0)),
            pl.BlockSpec((1, 2, nt, LANES, tile), lambda b, hp, i: (b, hp, 0, 0, 0)),
            pl.BlockSpec((1, seq, LANES), lambda b, hp, i: (b, 0, hp)),
        ],
        out_specs=pl.BlockSpec((1, tile, LANES), lambda b, hp, i: (b, i, hp)),
        out_shape=jax.ShapeDtypeStruct((batch, seq, D_ATT), bf16),
        scratch_shapes=[pltpu.VMEM((2, tile, LANES), f32), pltpu.VMEM((2, tile, 1), f32)],
        compiler_params=pltpu.CompilerParams(
            dimension_semantics=("arbitrary", "arbitrary", "arbitrary"),
            vmem_limit_bytes=VMEM_LIMIT),
        name="attention",
    )(qa, kta, v3)


def _conv_kernel(prev_ref, cur_ref, w_ref, b_ref, g_ref, beta_ref, o_ref, ext_ref, *, tile):
    i = pl.program_id(1)
    prev = prev_ref[0]
    ext_ref[0:CONV_HALO, :] = jnp.where(i == 0, jnp.zeros_like(prev), prev)
    ext_ref[CONV_HALO:, :] = cur_ref[0]
    shift = CONV_HALO - (CONV_WIDTH - 1)
    acc = jnp.zeros((tile, D_CONV), f32) + b_ref[...]
    for j in range(CONV_WIDTH):
        acc = acc + w_ref[j:j + 1, :] * ext_ref[shift + j:shift + j + tile, :]
    mu = jnp.mean(acc, axis=-1, keepdims=True)
    d = acc - mu
    var = jnp.mean(d * d, axis=-1, keepdims=True)
    y = d * lax.rsqrt(var + LN_EPS) * g_ref[...] + beta_ref[...]
    o_ref[0] = (y * (1.0 / (1.0 + jnp.exp(-y)))).astype(o_ref.dtype)


def _conv_module(u3, w_dw, b_dw, ln_g, ln_b, *, batch, seq):
    tile = ROW_TILE
    halo_blocks = tile // CONV_HALO
    w_pad = jnp.pad(w_dw, ((0, CONV_HALO - CONV_WIDTH), (0, 0)))
    vec = lambda: pl.BlockSpec((1, D_CONV), lambda b, i: (0, 0))
    return pl.pallas_call(
        functools.partial(_conv_kernel, tile=tile),
        grid=(batch, seq // tile),
        in_specs=[
            pl.BlockSpec((1, CONV_HALO, D_CONV),
                         lambda b, i: (b, jnp.maximum(i * halo_blocks - 1, 0), 0)),
            pl.BlockSpec((1, tile, D_CONV), lambda b, i: (b, i, 0)),
            pl.BlockSpec((CONV_HALO, D_CONV), lambda b, i: (0, 0)),
            vec(), vec(), vec(),
        ],
        out_specs=pl.BlockSpec((1, tile, D_CONV), lambda b, i: (b, i, 0)),
        out_shape=jax.ShapeDtypeStruct((batch, seq, D_CONV), bf16),
        scratch_shapes=[pltpu.VMEM((tile + CONV_HALO, D_CONV), f32)],
        compiler_params=pltpu.CompilerParams(
            dimension_semantics=("arbitrary", "arbitrary"), vmem_limit_bytes=VMEM_LIMIT),
        name="conv_module",
    )(u3, u3, w_pad, b_dw.reshape(1, D_CONV), ln_g.reshape(1, D_CONV), ln_b.reshape(1, D_CONV))


def _outproj_router_kernel(att_ref, conv_ref, x_ref, wo_ref, g_ref, wr_hi_ref, wr_lo_ref, br_ref,
                           triu_ref, h1_ref, hn_ref, idx_ref, w_ref, rank_ref, cnt_ref, carry_ref):
    i = pl.program_id(0)

    @pl.when(i == 0)
    def _():
        carry_ref[...] = jnp.zeros_like(carry_ref)

    h1 = (x_ref[...]
          + jnp.dot(att_ref[...], wo_ref[:D_ATT, :], preferred_element_type=f32)
          + jnp.dot(conv_ref[...], wo_ref[D_ATT:, :], preferred_element_type=f32))
    h1_ref[...] = h1
    hn = h1 * lax.rsqrt(jnp.mean(h1 * h1, axis=-1, keepdims=True) + RMS_EPS) * g_ref[...]
    hn_ref[...] = hn

    hn_hi = hn.astype(bf16)
    hn_lo = (hn - hn_hi.astype(f32)).astype(bf16)
    nt = (((1,), (1,)), ((), ()))
    logits = (lax.dot_general(wr_hi_ref[...], hn_hi, nt, preferred_element_type=f32)
              + lax.dot_general(wr_hi_ref[...], hn_lo, nt, preferred_element_type=f32)
              + lax.dot_general(wr_lo_ref[...], hn_hi, nt, preferred_element_type=f32)
              + br_ref[...])

    erow = lax.broadcasted_iota(jnp.int32, logits.shape, 0)
    work = logits
    vals, idxs = [], []
    for _ in range(TOP_K):
        mk = jnp.max(work, axis=0, keepdims=True)
        ik = jnp.min(jnp.where(work == mk, erow, N_EXPERTS), axis=0, keepdims=True)
        work = jnp.where(erow == ik, -jnp.inf, work)
        vals.append(mk)
        idxs.append(ik)
    exps = [jnp.exp(v - vals[0]) for v in vals]
    denom = exps[0] + exps[1] + exps[2] + exps[3]
    for k in range(TOP_K):
        w_ref[k:k + 1, :] = exps[k] / denom
        idx_ref[k:k + 1, :] = idxs[k]

    onehot = jnp.zeros(logits.shape, f32)
    for k in range(TOP_K):
        onehot = onehot + jnp.where(erow == idxs[k], 1.0, 0.0)
    cum = jnp.dot(onehot.astype(bf16), triu_ref[...], preferred_element_type=f32) + carry_ref[...]
    for k in range(TOP_K):
        rk = jnp.sum(jnp.where(erow == idxs[k], cum, 0.0), axis=0, keepdims=True)
        rank_ref[k:k + 1, :] = rk.astype(jnp.int32)
    total = carry_ref[...] + jnp.sum(onehot, axis=1, keepdims=True)
    carry_ref[...] = total
    cnt_ref[...] = jnp.broadcast_to(total, cnt_ref.shape)


def _outproj_router(att2d, conv2d, x2d, w_out, norm_ffn_g, w_router, b_router):
    n = x2d.shape[0]
    tm = ROW_TILE
    wr_t = w_router.T
    wr_hi = wr_t.astype(bf16)
    wr_lo = (wr_t - wr_hi.astype(f32)).astype(bf16)
    triu = jnp.asarray(np.triu(np.ones((tm, tm), np.float32), k=1), bf16)
    const = lambda shape: pl.BlockSpec(shape, lambda i: (0,) * len(shape))
    rows = lambda width: pl.BlockSpec((tm, width), lambda i: (i, 0))
    cols = lambda: pl.BlockSpec((TOP_K, tm), lambda i: (0, i))
    return pl.pallas_call(
        _outproj_router_kernel,
        grid=(n // tm,),
        in_specs=[
            rows(D_ATT), rows(D_CONV), rows(D_MODEL),
            const((D_MODEL, D_MODEL)), const((1, D_MODEL)),
            const((N_EXPERTS, D_MODEL)), const((N_EXPERTS, D_MODEL)), const((N_EXPERTS, 1)),
            const((tm, tm)),
        ],
        out_specs=[rows(D_MODEL), rows(D_MODEL), cols(), cols(), cols(), const((N_EXPERTS, LANES))],
        out_shape=[
            jax.ShapeDtypeStruct((n, D_MODEL), f32),
            jax.ShapeDtypeStruct((n, D_MODEL), f32),
            jax.ShapeDtypeStruct((TOP_K, n), jnp.int32),
            jax.ShapeDtypeStruct((TOP_K, n), f32),
            jax.ShapeDtypeStruct((TOP_K, n), jnp.int32),
            jax.ShapeDtypeStruct((N_EXPERTS, LANES), f32),
        ],
        scratch_shapes=[pltpu.VMEM((N_EXPERTS, 1), f32)],
        compiler_params=pltpu.CompilerParams(
            dimension_semantics=("arbitrary",), vmem_limit_bytes=VMEM_LIMIT),
        name="outproj_router",
    )(att2d, conv2d, x2d, w_out.astype(bf16), norm_ffn_g.reshape(1, D_MODEL), wr_hi, wr_lo,
      b_router.reshape(N_EXPERTS, 1), triu)


def _dispatch_kernel(dest_ref, hn_ref, xs_in_ref, xs_ref, sem, *, tile):
    del xs_in_ref

    def row_copy(r, k):
        return pltpu.make_async_copy(
            hn_ref.at[pl.ds(r, 1)], xs_ref.at[pl.ds(dest_ref[0, k, r], 1)], sem)

    def start(r, carry):
        for k in range(TOP_K):
            row_copy(r, k).start()
        return carry

    def wait(r, carry):
        for k in range(TOP_K):
            row_copy(r, k).wait()
        return carry

    lax.fori_loop(0, tile, start, 0)
    lax.fori_loop(0, tile, wait, 0)


def _dispatch(hn, dest, n_slots):
    n = hn.shape[0]
    tile = GATHER_TILE
    dest3 = dest.reshape(TOP_K, n // tile, tile).transpose(1, 0, 2)
    xs0 = jnp.zeros((n_slots, D_MODEL), f32)
    return pl.pallas_call(
        functools.partial(_dispatch_kernel, tile=tile),
        grid=(n // tile,),
        in_specs=[
            pl.BlockSpec((1, TOP_K, tile), lambda i: (i, 0, 0), memory_space=pltpu.SMEM),
            pl.BlockSpec((tile, D_MODEL), lambda i: (i, 0)),
            pl.BlockSpec(memory_space=pl.ANY),
        ],
        out_specs=pl.BlockSpec(memory_space=pl.ANY),
        out_shape=jax.ShapeDtypeStruct((n_slots, D_MODEL), f32),
        scratch_shapes=[pltpu.SemaphoreType.DMA],
        input_output_aliases={2: 0},
        compiler_params=pltpu.CompilerParams(
            dimension_semantics=("arbitrary",), vmem_limit_bytes=VMEM_LIMIT),
        name="dispatch",
    )(dest3, hn, xs0)


def _moe_kernel(te_ref, nu_ref, x_ref, wgu_ref, bgu_ref, wd_ref, bd_ref, y_ref, wgu_bf, wd_bf):
    i = pl.program_id(0)

    @pl.when(i >= nu_ref[0])
    def _():
        y_ref[...] = jnp.zeros_like(y_ref)

    @pl.when(i < nu_ref[0])
    def _():
        prev = te_ref[jnp.maximum(i - 1, 0)]

        @pl.when((i == 0) | (te_ref[i] != prev))
        def _():
            wgu_bf[...] = wgu_ref[0].astype(bf16)
            wd_bf[...] = wd_ref[0].astype(bf16)

        x = x_ref[...].astype(bf16)
        h = jnp.dot(x, wgu_bf[...], preferred_element_type=f32) + bgu_ref[0]
        hg = jnp.minimum(h[:, :D_FF], SWIGLU_LIMIT)
        hu = jnp.clip(h[:, D_FF:], -SWIGLU_LIMIT, SWIGLU_LIMIT)
        act = (hu + 1.0) * (hg * (1.0 / (1.0 + jnp.exp(-SWIGLU_ALPHA * hg))))
        y = jnp.dot(act.astype(bf16), wd_bf[...], preferred_element_type=f32) + bd_ref[0]
        y_ref[...] = y


def _moe(xs, tile_expert, n_used, w_gu, b_gu, w_down, b_down):
    n_slots = xs.shape[0]
    tg = GROUP_TILE
    grid_spec = pltpu.PrefetchScalarGridSpec(
        num_scalar_prefetch=2,
        grid=(n_slots // tg,),
        in_specs=[
            pl.BlockSpec((tg, D_MODEL), lambda i, te, nu: (i, 0)),
            pl.BlockSpec((1, D_MODEL, 2 * D_FF), lambda i, te, nu: (te[i], 0, 0)),
            pl.BlockSpec((1, 1, 2 * D_FF), lambda i, te, nu: (te[i], 0, 0)),
            pl.BlockSpec((1, D_FF, D_MODEL), lambda i, te, nu: (te[i], 0, 0)),
            pl.BlockSpec((1, 1, D_MODEL), lambda i, te, nu: (te[i], 0, 0)),
        ],
        out_specs=pl.BlockSpec((tg, D_MODEL), lambda i, te, nu: (i, 0)),
        scratch_shapes=[pltpu.VMEM((D_MODEL, 2 * D_FF), bf16), pltpu.VMEM((D_FF, D_MODEL), bf16)],
    )
    return pl.pallas_call(
        _moe_kernel,
        grid_spec=grid_spec,
        out_shape=jax.ShapeDtypeStruct((n_slots, D_MODEL), f32),
        compiler_params=pltpu.CompilerParams(
            dimension_semantics=("arbitrary",), vmem_limit_bytes=VMEM_LIMIT),
        name="moe_experts",
    )(tile_expert, n_used, xs, w_gu, b_gu.reshape(N_EXPERTS, 1, 2 * D_FF), w_down,
      b_down.reshape(N_EXPERTS, 1, D_MODEL))


def _combine_kernel(dest_ref, h1_ref, w_ref, g_ref, ys_ref, o_ref, buf_ref, sem, *, tile):
    def row_copy(r, k):
        return pltpu.make_async_copy(
            ys_ref.at[pl.ds(dest_ref[0, k, r], 1)], buf_ref.at[k, pl.ds(r, 1)], sem)

    def start(r, carry):
        for k in range(TOP_K):
            row_copy(r, k).start()
        return carry

    def wait(r, carry):
        for k in range(TOP_K):
            row_copy(r, k).wait()
        return carry

    lax.fori_loop(0, tile, start, 0)
    lax.fori_loop(0, tile, wait, 0)

    w = w_ref[...]
    h = h1_ref[...]
    for k in range(TOP_K):
        h = h + w[:, k:k + 1] * buf_ref[k]
    o_ref[...] = h * lax.rsqrt(jnp.mean(h * h, axis=-1, keepdims=True) + RMS_EPS) * g_ref[...]


def _combine(h1, ys, dest, top_w, norm_final_g):
    n = h1.shape[0]
    tile = GATHER_TILE
    dest3 = dest.reshape(TOP_K, n // tile, tile).transpose(1, 0, 2)
    return pl.pallas_call(
        functools.partial(_combine_kernel, tile=tile),
        grid=(n // tile,),
        in_specs=[
            pl.BlockSpec((1, TOP_K, tile), lambda i: (i, 0, 0), memory_space=pltpu.SMEM),
            pl.BlockSpec((tile, D_MODEL), lambda i: (i, 0)),
            pl.BlockSpec((tile, TOP_K), lambda i: (i, 0)),
            pl.BlockSpec((1, D_MODEL), lambda i: (0, 0)),
            pl.BlockSpec(memory_space=pl.ANY),
        ],
        out_specs=pl.BlockSpec((tile, D_MODEL), lambda i: (i, 0)),
        out_shape=jax.ShapeDtypeStruct((n, D_MODEL), f32),
        scratch_shapes=[pltpu.VMEM((TOP_K, tile, D_MODEL), f32), pltpu.SemaphoreType.DMA],
        compiler_params=pltpu.CompilerParams(
            dimension_semantics=("arbitrary",), vmem_limit_bytes=VMEM_LIMIT),
        name="combine",
    )(dest3, h1, top_w.T, norm_final_g.reshape(1, D_MODEL), ys)


def _routing_tables(top_idx, rank, counts, n_tiles):
    tg = GROUP_TILE
    cnt = counts[:, 0].astype(jnp.int32)
    padded = ((cnt + tg - 1) // tg) * tg
    ends = jnp.cumsum(padded)
    starts = ends - padded
    dest = starts[top_idx] + rank
    tile_ids = jnp.arange(n_tiles, dtype=jnp.int32)
    tile_expert = jnp.sum((tile_ids[:, None] * tg >= ends[None, :]).astype(jnp.int32), axis=1)
    tile_expert = jnp.minimum(tile_expert, N_EXPERTS - 1)
    n_used = (ends[-1] // tg).reshape(1)
    last_used = tile_expert[jnp.maximum(n_used[0] - 1, 0)]
    tile_expert = jnp.where(tile_ids < n_used[0], tile_expert, last_used)
    return dest, tile_expert, n_used


def kernel(x, norm_mix_g, w_in, b_f, b_glu, w_dw, b_dw, ln_g, ln_b, w_out, norm_ffn_g, w_router,
           b_router, w_gu, b_gu, w_down, b_down, norm_final_g):
    batch, seq, d = x.shape
    n = batch * seq
    x2d = x.reshape(n, d)

    qa, kta, v, u = _inproj(x2d, norm_mix_g, w_in, b_f, b_glu, batch=batch, seq=seq)
    att = _attention(qa, kta, v.reshape(batch, seq, D_ATT), batch=batch, seq=seq)
    conv = _conv_module(u.reshape(batch, seq, D_CONV), w_dw, b_dw, ln_g, ln_b, batch=batch, seq=seq)

    h1, hn, top_idx, top_w, rank, counts = _outproj_router(
        att.reshape(n, D_ATT), conv.reshape(n, D_CONV), x2d, w_out, norm_ffn_g, w_router, b_router)

    n_tiles = (n * TOP_K) // GROUP_TILE + N_EXPERTS
    dest, tile_expert, n_used = _routing_tables(top_idx, rank, counts, n_tiles)
    xs = _dispatch(hn, dest, n_tiles * GROUP_TILE)
    ys = _moe(xs, tile_expert, n_used, w_gu, b_gu, w_down, b_down)
    out = _combine(h1, ys, dest, top_w, norm_final_g)
    return out.reshape(batch, seq, d)
```

```python
import functools
import math

import numpy as np
import jax
import jax.numpy as jnp
from jax import lax
from jax.experimental import pallas as pl
from jax.experimental.pallas import tpu as pltpu

D_MODEL = 1024
ATT_HEADS = 8
HEAD_DIM = 64
D_ATT = ATT_HEADS * HEAD_DIM
D_CONV = D_MODEL - D_ATT
CONV_WIDTH = 31
N_EXPERTS = 32
TOP_K = 4
D_FF = 1024
SWIGLU_LIMIT = 7.0
SWIGLU_ALPHA = 1.702
RMS_EPS = 1e-6
LN_EPS = 1e-5
MASK_VALUE = -1e30

LANES = 128
ROW_TILE = 512
ATT_SUB = 256
CONV_HALO = 32
GROUP_TILE = 256
SLOT_ROWS = 16
X_WIDTH = D_MODEL + LANES
TILE_SLOT_ROWS = ROW_TILE * TOP_K + N_EXPERTS * SLOT_ROWS
PIECES_PER_TILE = TILE_SLOT_ROWS // SLOT_ROWS
PERM_CHUNK = 512
VMEM_LIMIT = 56 * 1024 * 1024

AUG_Q = HEAD_DIM
AUG_K = HEAD_DIM + 3
ONES_LANE = ATT_HEADS

f32 = jnp.float32
bf16 = jnp.bfloat16


def _split3(x):
    hi = x.astype(bf16)
    r1 = x - hi.astype(f32)
    mid = r1.astype(bf16)
    lo = (r1 - mid.astype(f32)).astype(bf16)
    return hi, mid, lo


def _selection_matrices():
    sel_q = np.zeros((3, LANES, ATT_HEADS * LANES), np.float32)
    sel_k = np.zeros((3, ATT_HEADS * LANES, LANES), np.float32)
    for h in range(ATT_HEADS):
        base = h * LANES
        for p in range(3):
            sel_q[p, h, base + AUG_Q + p] = 1.0
            sel_q[0, ONES_LANE, base + AUG_K + p] = 1.0
            sel_k[p, base + AUG_K + p, h] = -1.0
            sel_k[0, base + AUG_Q + p, ONES_LANE] = 1.0
    return jnp.asarray(sel_q, bf16), jnp.asarray(sel_k, bf16)


def _inproj_kernel(x_ref, g_ref, wq_ref, wkt_ref, wv_ref, wf_ref, wglu_ref, bf_ref, bglu_ref,
                   selq_ref, selk_ref, tri_ref,
                   qa_ref, kta_ref, v_ref, u_ref, carry_ref, *, tiles_per_seq):
    i = pl.program_id(0)

    @pl.when(i % tiles_per_seq == 0)
    def _():
        carry_ref[...] = jnp.zeros_like(carry_ref)

    x = x_ref[...]
    xn = x * lax.rsqrt(jnp.mean(x * x, axis=-1, keepdims=True) + RMS_EPS) * g_ref[...]
    xb = xn.astype(bf16)

    f = jnp.dot(xb, wf_ref[...], preferred_element_type=f32) + bf_ref[...]
    log_f = jnp.minimum(f, 0.0) - jnp.log1p(jnp.exp(-jnp.abs(f)))
    lane = lax.broadcasted_iota(jnp.int32, log_f.shape, 1)
    log_f = jnp.where(lane < ATT_HEADS, log_f, 0.0)
    tri = tri_ref[...]
    hi, mid, lo = _split3(log_f)
    c = (jnp.dot(tri, hi, preferred_element_type=f32)
         + jnp.dot(tri, mid, preferred_element_type=f32)
         + jnp.dot(tri, lo, preferred_element_type=f32)) + carry_ref[...]
    tm = c.shape[0]
    carry_ref[...] = c[tm - 1:tm, :]

    c_hi, c_mid, c_lo = _split3(c)
    c_hi = jnp.where(lane == ONES_LANE, jnp.ones_like(c_hi), c_hi)
    qa = jnp.dot(xb, wq_ref[...], preferred_element_type=f32) * (1.0 / math.sqrt(HEAD_DIM))
    qa = (qa + jnp.dot(c_hi, selq_ref[0], preferred_element_type=f32)
          + jnp.dot(c_mid, selq_ref[1], preferred_element_type=f32)
          + jnp.dot(c_lo, selq_ref[2], preferred_element_type=f32)).astype(bf16)
    for h in range(ATT_HEADS):
        qa_ref[0, h] = qa[:, h * LANES:(h + 1) * LANES]

    ct = c.T
    ct_hi, ct_mid, ct_lo = _split3(ct)
    row = lax.broadcasted_iota(jnp.int32, ct.shape, 0)
    ct_hi = jnp.where(row == ONES_LANE, jnp.ones_like(ct_hi), ct_hi)
    kta = lax.dot_general(wkt_ref[...], xb, (((1,), (1,)), ((), ())), preferred_element_type=f32)
    kta = (kta + jnp.dot(selk_ref[0], ct_hi, preferred_element_type=f32)
           + jnp.dot(selk_ref[1], ct_mid, preferred_element_type=f32)
           + jnp.dot(selk_ref[2], ct_lo, preferred_element_type=f32)).astype(bf16)
    for h in range(ATT_HEADS):
        kta_ref[0, h, 0] = kta[h * LANES:(h + 1) * LANES, :]

    v_ref[...] = jnp.dot(xb, wv_ref[...], preferred_element_type=f32).astype(bf16)

    glu = jnp.dot(xb, wglu_ref[...], preferred_element_type=f32) + bglu_ref[...]
    a = glu[:, :D_CONV]
    gate = glu[:, D_CONV:]
    u_ref[...] = a * (1.0 / (1.0 + jnp.exp(-gate)))


def _inproj(x2d, norm_mix_g, w_in, b_f, b_glu, *, batch, seq):
    n = batch * seq
    tm = ROW_TILE
    nt = seq // tm
    wq = w_in[:, :D_ATT]
    wk = w_in[:, D_ATT:2 * D_ATT]
    wv = w_in[:, 2 * D_ATT:3 * D_ATT]
    wf = w_in[:, 3 * D_ATT:3 * D_ATT + ATT_HEADS]
    wglu = w_in[:, 3 * D_ATT + ATT_HEADS:]
    pad = LANES - HEAD_DIM
    wq_pad = jnp.pad(wq.reshape(D_MODEL, ATT_HEADS, HEAD_DIM), ((0, 0), (0, 0), (0, pad)))
    wq_pad = wq_pad.reshape(D_MODEL, ATT_HEADS * LANES).astype(bf16)
    wkt_pad = jnp.pad(wk.T.reshape(ATT_HEADS, HEAD_DIM, D_MODEL), ((0, 0), (0, pad), (0, 0)))
    wkt_pad = wkt_pad.reshape(ATT_HEADS * LANES, D_MODEL).astype(bf16)
    wf_pad = jnp.pad(wf, ((0, 0), (0, LANES - ATT_HEADS))).astype(bf16)
    bf_pad = jnp.pad(b_f, (0, LANES - ATT_HEADS)).reshape(1, LANES)
    sel_q, sel_k = _selection_matrices()
    tri = jnp.asarray(np.tril(np.ones((tm, tm), np.float32)), bf16)

    const = lambda shape: pl.BlockSpec(shape, lambda i: (0,) * len(shape))
    return pl.pallas_call(
        functools.partial(_inproj_kernel, tiles_per_seq=nt),
        grid=(n // tm,),
        in_specs=[
            pl.BlockSpec((tm, D_MODEL), lambda i: (i, 0)),
            const((1, D_MODEL)),
            const((D_MODEL, ATT_HEADS * LANES)),
            const((ATT_HEADS * LANES, D_MODEL)),
            const((D_MODEL, D_ATT)),
            const((D_MODEL, LANES)),
            const((D_MODEL, 2 * D_CONV)),
            const((1, LANES)),
            const((1, 2 * D_CONV)),
            const((3, LANES, ATT_HEADS * LANES)),
            const((3, ATT_HEADS * LANES, LANES)),
            const((tm, tm)),
        ],
        out_specs=[
            pl.BlockSpec((1, ATT_HEADS, tm, LANES), lambda i: (i // nt, 0, i % nt, 0)),
            pl.BlockSpec((1, ATT_HEADS, 1, LANES, tm), lambda i: (i // nt, 0, i % nt, 0, 0)),
            pl.BlockSpec((tm, D_ATT), lambda i: (i, 0)),
            pl.BlockSpec((tm, D_CONV), lambda i: (i, 0)),
        ],
        out_shape=[
            jax.ShapeDtypeStruct((batch, ATT_HEADS, seq, LANES), bf16),
            jax.ShapeDtypeStruct((batch, ATT_HEADS, nt, LANES, tm), bf16),
            jax.ShapeDtypeStruct((n, D_ATT), bf16),
            jax.ShapeDtypeStruct((n, D_CONV), f32),
        ],
        scratch_shapes=[pltpu.VMEM((1, LANES), f32)],
        compiler_params=pltpu.CompilerParams(
            dimension_semantics=("arbitrary",), vmem_limit_bytes=VMEM_LIMIT),
        name="inproj",
    )(x2d, norm_mix_g.reshape(1, D_MODEL), wq_pad, wkt_pad, wv.astype(bf16), wf_pad,
      wglu.astype(bf16), bf_pad, b_glu.reshape(1, 2 * D_CONV), sel_q, sel_k, tri)


def _attn_kernel(qa_ref, kta_ref, v_ref, o_ref, acc_ref, m_ref, *, tile):
    i = pl.program_id(2)
    lane = lax.broadcasted_iota(jnp.int32, (tile, LANES), 1)
    first_half = lane < HEAD_DIM
    acc_ref[...] = jnp.zeros_like(acc_ref)
    m_ref[...] = jnp.full_like(m_ref, MASK_VALUE)
    sub = ATT_SUB

    def process(kv_tiles):
        v_augs = []
        for j, _ in kv_tiles:
            vp = v_ref[0, pl.ds(pl.multiple_of(j * tile, tile), tile), :]
            one = jnp.ones_like(vp)
            v_augs.append((jnp.where(first_half, vp, one), jnp.where(first_half, one, vp)))
        chains = [(t, h, rb) for t in range(len(kv_tiles)) for h in range(2)
                  for rb in range(tile // sub)]

        def scores(t, h, rb):
            j, diagonal = kv_tiles[t]
            width = (rb + 1) * sub if diagonal else tile
            return jnp.dot(qa_ref[0, h, rb * sub:(rb + 1) * sub, :], kta_ref[0, h, j][:, :width],
                           preferred_element_type=f32)

        s_next = scores(*chains[0])
        for ci, (t, h, rb) in enumerate(chains):
            s = s_next
            if ci + 1 < len(chains):
                s_next = scores(*chains[ci + 1])
            rows = slice(rb * sub, (rb + 1) * sub)
            width = s.shape[1]
            if kv_tiles[t][1]:
                r = lax.broadcasted_iota(jnp.int32, s.shape, 0) + rb * sub
                c = lax.broadcasted_iota(jnp.int32, s.shape, 1)
                s = jnp.where(c <= r, s, MASK_VALUE)
            m_old = m_ref[h, rows, :]
            m_new = jnp.maximum(m_old, jnp.max(s, axis=-1, keepdims=True))
            alpha = jnp.exp(m_old - m_new)
            p = jnp.exp(s - jnp.concatenate([m_new] * (width // LANES), axis=1)).astype(bf16)
            acc_ref[h, rows, :] = alpha * acc_ref[h, rows, :] + jnp.dot(
                p, v_augs[t][h][:width, :], preferred_element_type=f32)
            m_ref[h, rows, :] = m_new

    def body(jj, carry):
        process([(2 * jj, False), (2 * jj + 1, False)])
        return carry

    lax.fori_loop(0, i // 2, body, 0)

    @pl.when(i % 2 == 1)
    def _():
        process([(i - 1, False), (i, True)])

    @pl.when(i % 2 == 0)
    def _():
        process([(i, True)])

    a0 = acc_ref[0]
    a1 = acc_ref[1]
    l0 = a0[:, HEAD_DIM:HEAD_DIM + 1]
    l1 = a1[:, 0:1]
    o_ref[0] = jnp.where(first_half, a0 / l0, a1 / l1).astype(o_ref.dtype)


def _attention(qa, kta, v3, *, batch, seq):
    tile = ROW_TILE
    nt = seq // tile
    return pl.pallas_call(
        functools.partial(_attn_kernel, tile=tile),
        grid=(batch, ATT_HEADS // 2, nt),
        in_specs=[
            pl.BlockSpec((1, 2, tile, LANES), lambda b, hp, i: (b, hp, i, 0)),
            pl.BlockSpec((1, 2, nt, LANES, tile), lambda b, hp, i: (b, hp, 0, 0, 0)),
            pl.BlockSpec((1, seq, LANES), lambda b, hp, i: (b, 0, hp)),
        ],
        out_specs=pl.BlockSpec((1, tile, LANES), lambda b, hp, i: (b, i, hp)),
        out_shape=jax.ShapeDtypeStruct((batch, seq, D_ATT), bf16),
        scratch_shapes=[pltpu.VMEM((2, tile, LANES), f32), pltpu.VMEM((2, tile, LANES), f32)],
        compiler_params=pltpu.CompilerParams(
            dimension_semantics=("arbitrary", "arbitrary", "arbitrary"),
            vmem_limit_bytes=VMEM_LIMIT),
        name="attention",
    )(qa, kta, v3)


def _conv_kernel(prev_ref, cur_ref, w_ref, b_ref, g_ref, beta_ref, o_ref, ext_ref, *, tile):
    i = pl.program_id(1)
    prev = prev_ref[0]
    ext_ref[0:CONV_HALO, :] = jnp.where(i == 0, jnp.zeros_like(prev), prev)
    ext_ref[CONV_HALO:, :] = cur_ref[0]
    shift = CONV_HALO - (CONV_WIDTH - 1)
    acc = jnp.zeros((tile, D_CONV), f32) + b_ref[...]
    for j in range(CONV_WIDTH):
        acc = acc + w_ref[j:j + 1, :] * ext_ref[shift + j:shift + j + tile, :]
    mu = jnp.mean(acc, axis=-1, keepdims=True)
    d = acc - mu
    var = jnp.mean(d * d, axis=-1, keepdims=True)
    y = d * lax.rsqrt(var + LN_EPS) * g_ref[...] + beta_ref[...]
    o_ref[0] = (y * (1.0 / (1.0 + jnp.exp(-y)))).astype(o_ref.dtype)


def _conv_module(u3, w_dw, b_dw, ln_g, ln_b, *, batch, seq):
    tile = ROW_TILE
    halo_blocks = tile // CONV_HALO
    w_pad = jnp.pad(w_dw, ((0, CONV_HALO - CONV_WIDTH), (0, 0)))
    vec = lambda: pl.BlockSpec((1, D_CONV), lambda b, i: (0, 0))
    return pl.pallas_call(
        functools.partial(_conv_kernel, tile=tile),
        grid=(batch, seq // tile),
        in_specs=[
            pl.BlockSpec((1, CONV_HALO, D_CONV),
                         lambda b, i: (b, jnp.maximum(i * halo_blocks - 1, 0), 0)),
            pl.BlockSpec((1, tile, D_CONV), lambda b, i: (b, i, 0)),
            pl.BlockSpec((CONV_HALO, D_CONV), lambda b, i: (0, 0)),
            vec(), vec(), vec(),
        ],
        out_specs=pl.BlockSpec((1, tile, D_CONV), lambda b, i: (b, i, 0)),
        out_shape=jax.ShapeDtypeStruct((batch, seq, D_CONV), bf16),
        scratch_shapes=[pltpu.VMEM((tile + CONV_HALO, D_CONV), f32)],
        compiler_params=pltpu.CompilerParams(
            dimension_semantics=("arbitrary", "arbitrary"), vmem_limit_bytes=VMEM_LIMIT),
        name="conv_module",
    )(u3, u3, w_pad, b_dw.reshape(1, D_CONV), ln_g.reshape(1, D_CONV), ln_b.reshape(1, D_CONV))


def _outproj_router_kernel(att_ref, conv_ref, x_ref, wo_ref, g_ref, wr_hi_ref, wr_lo_ref, br_ref,
                           triu_ref, tril_ref, h1_ref, hna_ref, dl_ref, col_ref, cnt_ref):
    h1 = (x_ref[...]
          + jnp.dot(att_ref[...], wo_ref[:D_ATT, :], preferred_element_type=f32)
          + jnp.dot(conv_ref[...], wo_ref[D_ATT:, :], preferred_element_type=f32))
    h1_ref[...] = h1
    hn = h1 * lax.rsqrt(jnp.mean(h1 * h1, axis=-1, keepdims=True) + RMS_EPS) * g_ref[...]

    hn_hi = hn.astype(bf16)
    hn_lo = (hn - hn_hi.astype(f32)).astype(bf16)
    nt = (((1,), (1,)), ((), ()))
    logits = (lax.dot_general(wr_hi_ref[...], hn_hi, nt, preferred_element_type=f32)
              + lax.dot_general(wr_hi_ref[...], hn_lo, nt, preferred_element_type=f32)
              + lax.dot_general(wr_lo_ref[...], hn_hi, nt, preferred_element_type=f32)
              + br_ref[...])

    erow = lax.broadcasted_iota(jnp.int32, logits.shape, 0)
    work = logits
    vals, idxs = [], []
    for _ in range(TOP_K):
        mk = jnp.max(work, axis=0, keepdims=True)
        ik = jnp.min(jnp.where(work == mk, erow, N_EXPERTS), axis=0, keepdims=True)
        work = jnp.where(erow == ik, -jnp.inf, work)
        vals.append(mk)
        idxs.append(ik)
    exps = [jnp.exp(v - vals[0]) for v in vals]
    denom = exps[0] + exps[1] + exps[2] + exps[3]

    onehot = jnp.zeros(logits.shape, f32)
    gates = jnp.zeros(logits.shape, f32)
    for k in range(TOP_K):
        hit = erow == idxs[k]
        onehot = onehot + jnp.where(hit, 1.0, 0.0)
        gates = gates + jnp.where(hit, exps[k] / denom, 0.0)

    before = jnp.dot(onehot.astype(bf16), triu_ref[...], preferred_element_type=f32)
    count = jnp.sum(onehot, axis=1, keepdims=True)
    slots = jnp.floor((count + (SLOT_ROWS - 1)) * (1.0 / SLOT_ROWS))
    slots_b = jnp.broadcast_to(slots, (N_EXPERTS, LANES))
    slot_start = jnp.dot(tril_ref[...], slots_b.astype(bf16), preferred_element_type=f32)
    cnt_ref[0] = slots_b
    pos = before + slot_start[:, 0:1] * SLOT_ROWS
    dls = []
    for k in range(TOP_K):
        dk = jnp.sum(jnp.where(erow == idxs[k], pos, 0.0), axis=0, keepdims=True)
        dl_ref[k:k + 1, :] = dk.astype(jnp.int32)
        dls.append(dk)

    g_hi, g_mid, g_lo = _split3(gates)
    tm = logits.shape[1]
    r8 = lax.broadcasted_iota(jnp.int32, (8, tm), 0)
    dl8 = jnp.zeros((8, tm), f32)
    for k in range(TOP_K):
        dl8 = jnp.where(r8 == k, dls[k], dl8)
    stack = jnp.concatenate(
        [g_hi.astype(f32), g_mid.astype(f32), g_lo.astype(f32), dl8,
         jnp.zeros((LANES - 3 * N_EXPERTS - 8, tm), f32)], axis=0)
    col = stack.T
    col_ref[...] = col
    lane = lax.broadcasted_iota(jnp.int32, col.shape, 1)
    hna_ref[:, :D_MODEL] = hn_hi
    hna_ref[:, D_MODEL:] = jnp.where(lane < 3 * N_EXPERTS, col, 0.0).astype(bf16)


def _outproj_router(att2d, conv2d, x2d, w_out, norm_ffn_g, w_router, b_router):
    n = x2d.shape[0]
    tm = ROW_TILE
    wr_t = w_router.T
    wr_hi = wr_t.astype(bf16)
    wr_lo = (wr_t - wr_hi.astype(f32)).astype(bf16)
    triu = jnp.asarray(np.triu(np.ones((tm, tm), np.float32), k=1), bf16)
    tril = jnp.asarray(np.tril(np.ones((N_EXPERTS, N_EXPERTS), np.float32), k=-1), bf16)
    const = lambda shape: pl.BlockSpec(shape, lambda i: (0,) * len(shape))
    rows = lambda width: pl.BlockSpec((tm, width), lambda i: (i, 0))
    return pl.pallas_call(
        _outproj_router_kernel,
        grid=(n // tm,),
        in_specs=[
            rows(D_ATT), rows(D_CONV), rows(D_MODEL),
            const((D_MODEL, D_MODEL)), const((1, D_MODEL)),
            const((N_EXPERTS, D_MODEL)), const((N_EXPERTS, D_MODEL)), const((N_EXPERTS, 1)),
            const((tm, tm)), const((N_EXPERTS, N_EXPERTS)),
        ],
        out_specs=[
            rows(D_MODEL), rows(X_WIDTH),
            pl.BlockSpec((TOP_K, tm), lambda i: (0, i)),
            rows(LANES),
            pl.BlockSpec((1, N_EXPERTS, LANES), lambda i: (i, 0, 0)),
        ],
        out_shape=[
            jax.ShapeDtypeStruct((n, D_MODEL), f32),
            jax.ShapeDtypeStruct((n, X_WIDTH), bf16),
            jax.ShapeDtypeStruct((TOP_K, n), jnp.int32),
            jax.ShapeDtypeStruct((n, LANES), f32),
            jax.ShapeDtypeStruct((n // tm, N_EXPERTS, LANES), f32),
        ],
        compiler_params=pltpu.CompilerParams(
            dimension_semantics=("arbitrary",), vmem_limit_bytes=VMEM_LIMIT),
        name="outproj_router",
    )(att2d, conv2d, x2d, w_out.astype(bf16), norm_ffn_g.reshape(1, D_MODEL), wr_hi, wr_lo,
      b_router.reshape(N_EXPERTS, 1), triu, tril)


def _dispatch_kernel(pdest_ref, npiece_ref, padd_ref, npad_ref, nused_ref, dl_ref, hna_ref, xs_ref,
                     xbuf, zbuf, sems, zsem, *, n_tiles, n_group_tiles):
    i = pl.program_id(0)
    slot = i % 2

    def piece_copy(t, s, j):
        dst = pl.multiple_of(pdest_ref[t, j], SLOT_ROWS)
        return pltpu.make_async_copy(
            xbuf.at[s, pl.ds(pl.multiple_of(j * SLOT_ROWS, SLOT_ROWS), SLOT_ROWS)],
            xs_ref.at[pl.ds(dst, SLOT_ROWS)], sems.at[s])

    def pad_copy(m):
        dst = pl.multiple_of(padd_ref[m], SLOT_ROWS)
        return pltpu.make_async_copy(
            zbuf.at[pl.ds(0, SLOT_ROWS)], xs_ref.at[pl.ds(dst, SLOT_ROWS)], zsem)

    def tail_copy(t):
        dst = pl.multiple_of(t * GROUP_TILE, GROUP_TILE)
        return pltpu.make_async_copy(zbuf, xs_ref.at[pl.ds(dst, GROUP_TILE)], zsem)

    @pl.when(i >= 2)
    def _():
        def wait(j, c):
            piece_copy(i - 2, slot, j).wait()
            return c
        lax.fori_loop(0, npiece_ref[jnp.maximum(i - 2, 0)], wait, 0)

    @pl.when(i < n_tiles)
    def _():
        dl = dl_ref[...]
        rhs = hna_ref[...]
        for c in range(TILE_SLOT_ROWS // PERM_CHUNK):
            r = lax.broadcasted_iota(jnp.int32, (PERM_CHUNK, dl.shape[1]), 0) + c * PERM_CHUNK
            p = jnp.zeros(r.shape, f32)
            for k in range(TOP_K):
                p = jnp.where(dl[k:k + 1, :] == r, 1.0, p)
            xbuf[slot, c * PERM_CHUNK:(c + 1) * PERM_CHUNK, :] = jnp.dot(
                p.astype(bf16), rhs, preferred_element_type=f32).astype(bf16)

        def start(j, c):
            piece_copy(i, slot, j).start()
            return c
        lax.fori_loop(0, npiece_ref[jnp.minimum(i, n_tiles - 1)], start, 0)

    @pl.when(i == n_tiles)
    def _():
        zbuf[...] = jnp.zeros_like(zbuf)

        def start(m, c):
            pad_copy(m).start()
            return c
        lax.fori_loop(0, npad_ref[0], start, 0)

        def start_tail(t, c):
            tail_copy(t).start()
            return c
        lax.fori_loop(nused_ref[0], n_group_tiles, start_tail, 0)

    @pl.when(i == n_tiles + 1)
    def _():
        def wait(m, c):
            pad_copy(m).wait()
            return c
        lax.fori_loop(0, npad_ref[0], wait, 0)

        def wait_tail(t, c):
            tail_copy(t).wait()
            return c
        lax.fori_loop(nused_ref[0], n_group_tiles, wait_tail, 0)


def _dispatch(hna, dl, plan, n_rows):
    n = hna.shape[0]
    tm = ROW_TILE
    nt = n // tm
    last = nt - 1
    grid_spec = pltpu.PrefetchScalarGridSpec(
        num_scalar_prefetch=5,
        grid=(nt + 2,),
        in_specs=[
            pl.BlockSpec((TOP_K, tm), lambda i, *_: (0, jnp.minimum(i, last))),
            pl.BlockSpec((tm, X_WIDTH), lambda i, *_: (jnp.minimum(i, last), 0)),
        ],
        out_specs=pl.BlockSpec(memory_space=pl.ANY),
        scratch_shapes=[
            pltpu.VMEM((2, TILE_SLOT_ROWS, X_WIDTH), bf16),
            pltpu.VMEM((GROUP_TILE, X_WIDTH), bf16),
            pltpu.SemaphoreType.DMA((2,)),
            pltpu.SemaphoreType.DMA,
        ],
    )
    return pl.pallas_call(
        functools.partial(_dispatch_kernel, n_tiles=nt, n_group_tiles=n_rows // GROUP_TILE),
        grid_spec=grid_spec,
        out_shape=jax.ShapeDtypeStruct((n_rows, X_WIDTH), bf16),
        compiler_params=pltpu.CompilerParams(
            dimension_semantics=("arbitrary",), vmem_limit_bytes=VMEM_LIMIT),
        name="dispatch",
    )(plan["piece_dest"], plan["n_pieces"], plan["pad_dest"], plan["n_pad"], plan["n_used"], dl, hna)


def _moe_kernel(te_ref, nu_ref, x_ref, wgu_ref, bgu_ref, wd_ref, bd_ref, y_ref, wgu_bf, wd_bf):
    i = pl.program_id(0)

    @pl.when(i >= nu_ref[0])
    def _():
        y_ref[...] = jnp.zeros_like(y_ref)

    @pl.when(i < nu_ref[0])
    def _():
        e = te_ref[i]
        prev = te_ref[jnp.maximum(i - 1, 0)]

        @pl.when((i == 0) | (e != prev))
        def _():
            wgu_bf[...] = wgu_ref[0].astype(bf16)
            wd_bf[...] = wd_ref[0].astype(bf16)

        h = jnp.dot(x_ref[:, :D_MODEL], wgu_bf[...], preferred_element_type=f32) + bgu_ref[0]
        hg = jnp.minimum(h[:, :D_FF], SWIGLU_LIMIT)
        hu = jnp.clip(h[:, D_FF:], -SWIGLU_LIMIT, SWIGLU_LIMIT)
        act = (hu + 1.0) * (hg * (1.0 / (1.0 + jnp.exp(-SWIGLU_ALPHA * hg))))
        y = jnp.dot(act.astype(bf16), wd_bf[...], preferred_element_type=f32) + bd_ref[0]
        aug = x_ref[:, D_MODEL:].astype(f32)
        lane = lax.broadcasted_iota(jnp.int32, aug.shape, 1)
        gate = jnp.sum(jnp.where((lane & (N_EXPERTS - 1)) == e, aug, 0.0), axis=1, keepdims=True)
        y_ref[...] = (gate * y).astype(y_ref.dtype)


def _moe(xs, plan, w_gu, b_gu, w_down, b_down):
    n_rows = xs.shape[0]
    tg = GROUP_TILE
    grid_spec = pltpu.PrefetchScalarGridSpec(
        num_scalar_prefetch=2,
        grid=(n_rows // tg,),
        in_specs=[
            pl.BlockSpec((tg, X_WIDTH), lambda i, te, nu: (jnp.minimum(i, nu[0] - 1), 0)),
            pl.BlockSpec((1, D_MODEL, 2 * D_FF), lambda i, te, nu: (te[i], 0, 0)),
            pl.BlockSpec((1, 1, 2 * D_FF), lambda i, te, nu: (te[i], 0, 0)),
            pl.BlockSpec((1, D_FF, D_MODEL), lambda i, te, nu: (te[i], 0, 0)),
            pl.BlockSpec((1, 1, D_MODEL), lambda i, te, nu: (te[i], 0, 0)),
        ],
        out_specs=pl.BlockSpec((tg, D_MODEL), lambda i, te, nu: (i, 0)),
        scratch_shapes=[pltpu.VMEM((D_MODEL, 2 * D_FF), bf16), pltpu.VMEM((D_FF, D_MODEL), bf16)],
    )
    return pl.pallas_call(
        _moe_kernel,
        grid_spec=grid_spec,
        out_shape=jax.ShapeDtypeStruct((n_rows, D_MODEL), bf16),
        compiler_params=pltpu.CompilerParams(
            dimension_semantics=("arbitrary",), vmem_limit_bytes=VMEM_LIMIT),
        name="moe_experts",
    )(plan["tile_expert"], plan["n_used"], xs, w_gu, b_gu.reshape(N_EXPERTS, 1, 2 * D_FF), w_down,
      b_down.reshape(N_EXPERTS, 1, D_MODEL))


def _combine_kernel(pdest_ref, npiece_ref, h1_ref, col_ref, g_ref, ys_ref, o_ref, ybuf, sems,
                    *, n_tiles):
    i = pl.program_id(0)
    slot = i % 2

    def piece_copy(t, s, j):
        src = pl.multiple_of(pdest_ref[t, j], SLOT_ROWS)
        return pltpu.make_async_copy(
            ys_ref.at[pl.ds(src, SLOT_ROWS)],
            ybuf.at[s, pl.ds(pl.multiple_of(j * SLOT_ROWS, SLOT_ROWS), SLOT_ROWS)], sems.at[s])

    def fetch(t, s):
        def start(j, c):
            piece_copy(t, s, j).start()
            return c
        lax.fori_loop(0, npiece_ref[t], start, 0)

    @pl.when(i == 0)
    def _():
        ybuf[...] = jnp.zeros_like(ybuf)
        fetch(0, 0)

    @pl.when(i + 1 < n_tiles)
    def _():
        fetch(jnp.minimum(i + 1, n_tiles - 1), 1 - slot)

    def wait(j, c):
        piece_copy(i, slot, j).wait()
        return c
    lax.fori_loop(0, npiece_ref[i], wait, 0)

    col = col_ref[...]
    rows = [col[:, 3 * N_EXPERTS + k:3 * N_EXPERTS + k + 1].astype(jnp.int32) for k in range(TOP_K)]
    h = h1_ref[...]
    for c in range(TILE_SLOT_ROWS // PERM_CHUNK):
        r = lax.broadcasted_iota(jnp.int32, (col.shape[0], PERM_CHUNK), 1) + c * PERM_CHUNK
        g = jnp.zeros(r.shape, f32)
        for k in range(TOP_K):
            g = jnp.where(rows[k] == r, 1.0, g)
        h = h + jnp.dot(g.astype(bf16), ybuf[slot, c * PERM_CHUNK:(c + 1) * PERM_CHUNK, :],
                        preferred_element_type=f32)
    o_ref[...] = h * lax.rsqrt(jnp.mean(h * h, axis=-1, keepdims=True) + RMS_EPS) * g_ref[...]


def _combine(h1, ys, col, plan, norm_final_g):
    n = h1.shape[0]
    tm = ROW_TILE
    nt = n // tm
    grid_spec = pltpu.PrefetchScalarGridSpec(
        num_scalar_prefetch=2,
        grid=(nt,),
        in_specs=[
            pl.BlockSpec((tm, D_MODEL), lambda i, *_: (i, 0)),
            pl.BlockSpec((tm, LANES), lambda i, *_: (i, 0)),
            pl.BlockSpec((1, D_MODEL), lambda i, *_: (0, 0)),
            pl.BlockSpec(memory_space=pl.ANY),
        ],
        out_specs=pl.BlockSpec((tm, D_MODEL), lambda i, *_: (i, 0)),
        scratch_shapes=[
            pltpu.VMEM((2, TILE_SLOT_ROWS, D_MODEL), bf16),
            pltpu.SemaphoreType.DMA((2,)),
        ],
    )
    return pl.pallas_call(
        functools.partial(_combine_kernel, n_tiles=nt),
        grid_spec=grid_spec,
        out_shape=jax.ShapeDtypeStruct((n, D_MODEL), f32),
        compiler_params=pltpu.CompilerParams(
            dimension_semantics=("arbitrary",), vmem_limit_bytes=VMEM_LIMIT),
        name="combine",
    )(plan["piece_dest"], plan["n_pieces"], h1, col, norm_final_g.reshape(1, D_MODEL), ys)


def _routing_plan(slot_counts, n_group_tiles):
    tg = GROUP_TILE
    i32 = jnp.int32
    pc = slot_counts.astype(i32) * SLOT_ROWS
    local_end = jnp.cumsum(pc, axis=1)
    local_start = local_end - pc
    total = jnp.sum(pc, axis=0)
    region = ((total + tg - 1) // tg) * tg
    region_end = jnp.cumsum(region)
    region_start = region_end - region
    base = region_start[None, :] + jnp.cumsum(pc, axis=0) - pc
    n_pieces = local_end[:, -1] // SLOT_ROWS

    piece_row = jnp.arange(PIECES_PER_TILE, dtype=i32) * SLOT_ROWS
    owner = jnp.sum((local_end[:, None, :] <= piece_row[None, :, None]).astype(i32), axis=2)
    owner_hit = owner[:, :, None] == jnp.arange(N_EXPERTS, dtype=i32)[None, None, :]
    shift = jnp.sum(jnp.where(owner_hit, (base - local_start)[:, None, :], 0), axis=2)
    piece_dest = shift + piece_row[None, :]
    piece_dest = jnp.where(piece_row[None, :] < local_end[:, -1:], piece_dest, 0)

    pad_slots = (region - total) // SLOT_ROWS
    pad_end = jnp.cumsum(pad_slots)
    m = jnp.arange(N_EXPERTS * (tg // SLOT_ROWS), dtype=i32)
    pad_owner = jnp.minimum(jnp.sum((pad_end[None, :] <= m[:, None]).astype(i32), axis=1),
                            N_EXPERTS - 1)
    pad_hit = pad_owner[:, None] == jnp.arange(N_EXPERTS, dtype=i32)[None, :]
    pad_first = jnp.sum(jnp.where(pad_hit, (region_start + total - (pad_end - pad_slots) * SLOT_ROWS)
                                  [None, :], 0), axis=1)
    pad_dest = jnp.where(m < pad_end[-1], pad_first + m * SLOT_ROWS, 0)

    tile_row = jnp.arange(n_group_tiles, dtype=i32) * tg
    tile_expert = jnp.sum((region_end[None, :] <= tile_row[:, None]).astype(i32), axis=1)
    n_used = region_end[-1] // tg
    last_expert = jnp.sum((region_end <= (n_used - 1) * tg).astype(i32))
    tile_expert = jnp.where(tile_row < region_end[-1], tile_expert, last_expert)
    return {
        "piece_dest": piece_dest, "n_pieces": n_pieces, "pad_dest": pad_dest,
        "n_pad": pad_end[-1:], "tile_expert": tile_expert, "n_used": n_used.reshape(1),
    }


def kernel(x, norm_mix_g, w_in, b_f, b_glu, w_dw, b_dw, ln_g, ln_b, w_out, norm_ffn_g, w_router,
           b_router, w_gu, b_gu, w_down, b_down, norm_final_g):
    batch, seq, d = x.shape
    n = batch * seq
    x2d = x.reshape(n, d)

    qa, kta, v, u = _inproj(x2d, norm_mix_g, w_in, b_f, b_glu, batch=batch, seq=seq)
    att = _attention(qa, kta, v.reshape(batch, seq, D_ATT), batch=batch, seq=seq)
    conv = _conv_module(u.reshape(batch, seq, D_CONV), w_dw, b_dw, ln_g, ln_b, batch=batch, seq=seq)

    h1, hna, dl, col, slot_counts = _outproj_router(
        att.reshape(n, D_ATT), conv.reshape(n, D_CONV), x2d, w_out, norm_ffn_g, w_router, b_router)

    n_token_tiles = n // ROW_TILE
    max_rows = n * TOP_K + N_EXPERTS * (n_token_tiles * (SLOT_ROWS - 1) + GROUP_TILE - 1)
    n_group_tiles = -(-max_rows // GROUP_TILE)
    plan = _routing_plan(slot_counts[:, :, 0], n_group_tiles)
    xs = _dispatch(hna, dl, plan, n_group_tiles * GROUP_TILE)
    ys = _moe(xs, plan, w_gu, b_gu, w_down, b_down)
    out = _combine(h1, ys, col, plan, norm_final_g)
    return out.reshape(batch, seq, d)
```

```python
import functools
import math

import numpy as np
import jax
import jax.numpy as jnp
from jax import lax
from jax.experimental import pallas as pl
from jax.experimental.pallas import tpu as pltpu

D_MODEL = 1024
ATT_HEADS = 8
HEAD_DIM = 64
D_ATT = ATT_HEADS * HEAD_DIM
D_CONV = D_MODEL - D_ATT
CONV_WIDTH = 31
N_EXPERTS = 32
TOP_K = 4
D_FF = 1024
SWIGLU_LIMIT = 7.0
SWIGLU_ALPHA = 1.702
RMS_EPS = 1e-6
LN_EPS = 1e-5
MASK_VALUE = -1e30

LANES = 128
SUBLANES = 8
CONV_ROWS = 64
ROW_TILE = 512
ATT_SUB = 256
CONV_HALO = 32
GROUP_TILE = 256
SLOT_ROWS = 16
X_WIDTH = D_MODEL + LANES
TILE_SLOT_ROWS = ROW_TILE * TOP_K + N_EXPERTS * SLOT_ROWS
PIECES_PER_TILE = TILE_SLOT_ROWS // SLOT_ROWS
PERM_CHUNK = 512
VMEM_LIMIT = 56 * 1024 * 1024

AUG_Q = HEAD_DIM
AUG_K = HEAD_DIM + 3
ONES_LANE = ATT_HEADS

f32 = jnp.float32
bf16 = jnp.bfloat16


def _split3(x):
    hi = x.astype(bf16)
    r1 = x - hi.astype(f32)
    mid = r1.astype(bf16)
    lo = (r1 - mid.astype(f32)).astype(bf16)
    return hi, mid, lo


def _selection_matrices():
    sel_q = np.zeros((3, LANES, ATT_HEADS * LANES), np.float32)
    sel_k = np.zeros((3, ATT_HEADS * LANES, LANES), np.float32)
    for h in range(ATT_HEADS):
        base = h * LANES
        for p in range(3):
            sel_q[p, h, base + AUG_Q + p] = 1.0
            sel_q[0, ONES_LANE, base + AUG_K + p] = 1.0
            sel_k[p, base + AUG_K + p, h] = -1.0
            sel_k[0, base + AUG_Q + p, ONES_LANE] = 1.0
    return jnp.asarray(sel_q, bf16), jnp.asarray(sel_k, bf16)


def _inproj_kernel(x_ref, g_ref, wq_ref, wkt_ref, wv_ref, wf_ref, wglu_ref, bf_ref, bglu_ref,
                   selq_ref, selk_ref, tri_ref,
                   qa_ref, kta_ref, v_ref, u_ref, carry_ref, *, tiles_per_seq):
    i = pl.program_id(0)

    @pl.when(i % tiles_per_seq == 0)
    def _():
        carry_ref[...] = jnp.zeros_like(carry_ref)

    x = x_ref[...]
    xn = x * lax.rsqrt(jnp.mean(x * x, axis=-1, keepdims=True) + RMS_EPS) * g_ref[...]
    xb = xn.astype(bf16)

    f = jnp.dot(xb, wf_ref[...], preferred_element_type=f32) + bf_ref[...]
    log_f = jnp.minimum(f, 0.0) - jnp.log1p(jnp.exp(-jnp.abs(f)))
    lane = lax.broadcasted_iota(jnp.int32, log_f.shape, 1)
    log_f = jnp.where(lane < ATT_HEADS, log_f, 0.0)
    tri = tri_ref[...]
    hi, mid, lo = _split3(log_f)
    c = (jnp.dot(tri, hi, preferred_element_type=f32)
         + jnp.dot(tri, mid, preferred_element_type=f32)
         + jnp.dot(tri, lo, preferred_element_type=f32)) + carry_ref[...]
    tm = c.shape[0]
    carry_ref[...] = c[tm - 1:tm, :]

    c_hi, c_mid, c_lo = _split3(c)
    c_hi = jnp.where(lane == ONES_LANE, jnp.ones_like(c_hi), c_hi)
    qa = jnp.dot(xb, wq_ref[...], preferred_element_type=f32) * (1.0 / math.sqrt(HEAD_DIM))
    qa = (qa + jnp.dot(c_hi, selq_ref[0], preferred_element_type=f32)
          + jnp.dot(c_mid, selq_ref[1], preferred_element_type=f32)
          + jnp.dot(c_lo, selq_ref[2], preferred_element_type=f32)).astype(bf16)
    for h in range(ATT_HEADS):
        qa_ref[0, h] = qa[:, h * LANES:(h + 1) * LANES]

    ct = c.T
    ct_hi, ct_mid, ct_lo = _split3(ct)
    row = lax.broadcasted_iota(jnp.int32, ct.shape, 0)
    ct_hi = jnp.where(row == ONES_LANE, jnp.ones_like(ct_hi), ct_hi)
    kta = lax.dot_general(wkt_ref[...], xb, (((1,), (1,)), ((), ())), preferred_element_type=f32)
    kta = (kta + jnp.dot(selk_ref[0], ct_hi, preferred_element_type=f32)
           + jnp.dot(selk_ref[1], ct_mid, preferred_element_type=f32)
           + jnp.dot(selk_ref[2], ct_lo, preferred_element_type=f32)).astype(bf16)
    for h in range(ATT_HEADS):
        kta_ref[0, h, 0] = kta[h * LANES:(h + 1) * LANES, :]

    v_ref[...] = jnp.dot(xb, wv_ref[...], preferred_element_type=f32).astype(bf16)

    glu = jnp.dot(xb, wglu_ref[...], preferred_element_type=f32) + bglu_ref[...]
    a = glu[:, :D_CONV]
    gate = glu[:, D_CONV:]
    u_ref[...] = a * (1.0 / (1.0 + jnp.exp(-gate)))


def _inproj(x2d, norm_mix_g, w_in, b_f, b_glu, *, batch, seq):
    n = batch * seq
    tm = ROW_TILE
    nt = seq // tm
    wq = w_in[:, :D_ATT]
    wk = w_in[:, D_ATT:2 * D_ATT]
    wv = w_in[:, 2 * D_ATT:3 * D_ATT]
    wf = w_in[:, 3 * D_ATT:3 * D_ATT + ATT_HEADS]
    wglu = w_in[:, 3 * D_ATT + ATT_HEADS:]
    pad = LANES - HEAD_DIM
    wq_pad = jnp.pad(wq.reshape(D_MODEL, ATT_HEADS, HEAD_DIM), ((0, 0), (0, 0), (0, pad)))
    wq_pad = wq_pad.reshape(D_MODEL, ATT_HEADS * LANES).astype(bf16)
    wkt_pad = jnp.pad(wk.T.reshape(ATT_HEADS, HEAD_DIM, D_MODEL), ((0, 0), (0, pad), (0, 0)))
    wkt_pad = wkt_pad.reshape(ATT_HEADS * LANES, D_MODEL).astype(bf16)
    wf_pad = jnp.pad(wf, ((0, 0), (0, LANES - ATT_HEADS))).astype(bf16)
    bf_pad = jnp.pad(b_f, (0, LANES - ATT_HEADS)).reshape(1, LANES)
    sel_q, sel_k = _selection_matrices()
    tri = jnp.asarray(np.tril(np.ones((tm, tm), np.float32)), bf16)

    const = lambda shape: pl.BlockSpec(shape, lambda i: (0,) * len(shape))
    return pl.pallas_call(
        functools.partial(_inproj_kernel, tiles_per_seq=nt),
        grid=(n // tm,),
        in_specs=[
            pl.BlockSpec((tm, D_MODEL), lambda i: (i, 0)),
            const((1, D_MODEL)),
            const((D_MODEL, ATT_HEADS * LANES)),
            const((ATT_HEADS * LANES, D_MODEL)),
            const((D_MODEL, D_ATT)),
            const((D_MODEL, LANES)),
            const((D_MODEL, 2 * D_CONV)),
            const((1, LANES)),
            const((1, 2 * D_CONV)),
            const((3, LANES, ATT_HEADS * LANES)),
            const((3, ATT_HEADS * LANES, LANES)),
            const((tm, tm)),
        ],
        out_specs=[
            pl.BlockSpec((1, ATT_HEADS, tm, LANES), lambda i: (i // nt, 0, i % nt, 0)),
            pl.BlockSpec((1, ATT_HEADS, 1, LANES, tm), lambda i: (i // nt, 0, i % nt, 0, 0)),
            pl.BlockSpec((tm, D_ATT), lambda i: (i, 0)),
            pl.BlockSpec((tm, D_CONV), lambda i: (i, 0)),
        ],
        out_shape=[
            jax.ShapeDtypeStruct((batch, ATT_HEADS, seq, LANES), bf16),
            jax.ShapeDtypeStruct((batch, ATT_HEADS, nt, LANES, tm), bf16),
            jax.ShapeDtypeStruct((n, D_ATT), bf16),
            jax.ShapeDtypeStruct((n, D_CONV), f32),
        ],
        scratch_shapes=[pltpu.VMEM((1, LANES), f32)],
        compiler_params=pltpu.CompilerParams(
            dimension_semantics=("arbitrary",), vmem_limit_bytes=VMEM_LIMIT),
        name="inproj",
    )(x2d, norm_mix_g.reshape(1, D_MODEL), wq_pad, wkt_pad, wv.astype(bf16), wf_pad,
      wglu.astype(bf16), bf_pad, b_glu.reshape(1, 2 * D_CONV), sel_q, sel_k, tri)


def _attn_kernel(qa_ref, kta_ref, v_ref, o_ref, acc_ref, m_ref, *, tile):
    i = pl.program_id(2)
    lane = lax.broadcasted_iota(jnp.int32, (tile, LANES), 1)
    first_half = lane < HEAD_DIM
    acc_ref[...] = jnp.zeros_like(acc_ref)
    m_ref[...] = jnp.full_like(m_ref, MASK_VALUE)
    sub = ATT_SUB

    def process(kv_tiles):
        v_augs = []
        for j, _ in kv_tiles:
            vp = v_ref[0, pl.ds(pl.multiple_of(j * tile, tile), tile), :]
            one = jnp.ones_like(vp)
            v_augs.append((jnp.where(first_half, vp, one), jnp.where(first_half, one, vp)))
        chains = [(t, h, rb) for t in range(len(kv_tiles)) for h in range(2)
                  for rb in range(tile // sub)]

        def scores(t, h, rb):
            j, diagonal = kv_tiles[t]
            width = (rb + 1) * sub if diagonal else tile
            return jnp.dot(qa_ref[0, h, rb * sub:(rb + 1) * sub, :], kta_ref[0, h, j][:, :width],
                           preferred_element_type=f32)

        s_next = scores(*chains[0])
        for ci, (t, h, rb) in enumerate(chains):
            s = s_next
            if ci + 1 < len(chains):
                s_next = scores(*chains[ci + 1])
            rows = slice(rb * sub, (rb + 1) * sub)
            width = s.shape[1]
            if kv_tiles[t][1]:
                r = lax.broadcasted_iota(jnp.int32, s.shape, 0) + rb * sub
                c = lax.broadcasted_iota(jnp.int32, s.shape, 1)
                s = jnp.where(c <= r, s, MASK_VALUE)
            m_old = m_ref[h, rows, :]
            m_new = jnp.maximum(m_old, jnp.max(s, axis=-1, keepdims=True))
            alpha = jnp.exp(m_old - m_new)
            p = jnp.exp(s - jnp.concatenate([m_new] * (width // LANES), axis=1)).astype(bf16)
            acc_ref[h, rows, :] = alpha * acc_ref[h, rows, :] + jnp.dot(
                p, v_augs[t][h][:width, :], preferred_element_type=f32)
            m_ref[h, rows, :] = m_new

    def body(jj, carry):
        process([(2 * jj, False), (2 * jj + 1, False)])
        return carry

    lax.fori_loop(0, i // 2, body, 0)

    @pl.when(i % 2 == 1)
    def _():
        process([(i - 1, False), (i, True)])

    @pl.when(i % 2 == 0)
    def _():
        process([(i, True)])

    a0 = acc_ref[0]
    a1 = acc_ref[1]
    l0 = a0[:, HEAD_DIM:HEAD_DIM + 1]
    l1 = a1[:, 0:1]
    o_ref[0] = jnp.where(first_half, a0 / l0, a1 / l1).astype(o_ref.dtype)


def _attention(qa, kta, v3, *, batch, seq):
    tile = ROW_TILE
    nt = seq // tile
    return pl.pallas_call(
        functools.partial(_attn_kernel, tile=tile),
        grid=(batch, ATT_HEADS // 2, nt),
        in_specs=[
            pl.BlockSpec((1, 2, tile, LANES), lambda b, hp, i: (b, hp, i, 0)),
            pl.BlockSpec((1, 2, nt, LANES, tile), lambda b, hp, i: (b, hp, 0, 0, 0)),
            pl.BlockSpec((1, seq, LANES), lambda b, hp, i: (b, 0, hp)),
        ],
        out_specs=pl.BlockSpec((1, tile, LANES), lambda b, hp, i: (b, i, hp)),
        out_shape=jax.ShapeDtypeStruct((batch, seq, D_ATT), bf16),
        scratch_shapes=[pltpu.VMEM((2, tile, LANES), f32), pltpu.VMEM((2, tile, LANES), f32)],
        compiler_params=pltpu.CompilerParams(
            dimension_semantics=("arbitrary", "arbitrary", "arbitrary"),
            vmem_limit_bytes=VMEM_LIMIT),
        name="attention",
    )(qa, kta, v3)


def _conv_kernel(prev_ref, cur_ref, w_ref, b_ref, g_ref, beta_ref, o_ref, ext_ref, *, tile):
    i = pl.program_id(1)
    prev = prev_ref[0]
    ext_ref[0, 0:CONV_HALO, :] = jnp.where(i == 0, jnp.zeros_like(prev), prev)
    ext_ref[0, CONV_HALO:, :] = cur_ref[0]
    span = tile + CONV_HALO - SUBLANES
    for s in range(1, SUBLANES):
        ext_ref[s, 0:span, :] = ext_ref[0, s:s + span, :]
    shift = CONV_HALO - (CONV_WIDTH - 1)

    def rows_block(cb, carry):
        r0 = pl.multiple_of(cb * CONV_ROWS, CONV_ROWS)
        acc = jnp.zeros((CONV_ROWS, D_CONV), f32) + b_ref[...]
        for j in range(CONV_WIDTH):
            s = (shift + j) % SUBLANES
            a = shift + j - s
            acc = acc + w_ref[j:j + 1, :] * ext_ref[s, pl.ds(r0 + a, CONV_ROWS), :]
        mu = jnp.mean(acc, axis=-1, keepdims=True)
        d = acc - mu
        var = jnp.mean(d * d, axis=-1, keepdims=True)
        y = d * lax.rsqrt(var + LN_EPS) * g_ref[...] + beta_ref[...]
        o_ref[0, pl.ds(r0, CONV_ROWS), :] = (y * (1.0 / (1.0 + jnp.exp(-y)))).astype(o_ref.dtype)
        return carry

    lax.fori_loop(0, tile // CONV_ROWS, rows_block, 0)


def _conv_module(u3, w_dw, b_dw, ln_g, ln_b, *, batch, seq):
    tile = ROW_TILE
    halo_blocks = tile // CONV_HALO
    w_pad = jnp.pad(w_dw, ((0, CONV_HALO - CONV_WIDTH), (0, 0)))
    vec = lambda: pl.BlockSpec((1, D_CONV), lambda b, i: (0, 0))
    return pl.pallas_call(
        functools.partial(_conv_kernel, tile=tile),
        grid=(batch, seq // tile),
        in_specs=[
            pl.BlockSpec((1, CONV_HALO, D_CONV),
                         lambda b, i: (b, jnp.maximum(i * halo_blocks - 1, 0), 0)),
            pl.BlockSpec((1, tile, D_CONV), lambda b, i: (b, i, 0)),
            pl.BlockSpec((CONV_HALO, D_CONV), lambda b, i: (0, 0)),
            vec(), vec(), vec(),
        ],
        out_specs=pl.BlockSpec((1, tile, D_CONV), lambda b, i: (b, i, 0)),
        out_shape=jax.ShapeDtypeStruct((batch, seq, D_CONV), bf16),
        scratch_shapes=[pltpu.VMEM((SUBLANES, tile + CONV_HALO, D_CONV), f32)],
        compiler_params=pltpu.CompilerParams(
            dimension_semantics=("arbitrary", "arbitrary"), vmem_limit_bytes=VMEM_LIMIT),
        name="conv_module",
    )(u3, u3, w_pad, b_dw.reshape(1, D_CONV), ln_g.reshape(1, D_CONV), ln_b.reshape(1, D_CONV))


def _outproj_router_kernel(att_ref, conv_ref, x_ref, wo_ref, g_ref, wr_hi_ref, wr_lo_ref, br_ref,
                           triu_ref, tril_ref, h1_ref, hna_ref, dl_ref, col_ref, cnt_ref):
    h1 = (x_ref[...]
          + jnp.dot(att_ref[...], wo_ref[:D_ATT, :], preferred_element_type=f32)
          + jnp.dot(conv_ref[...], wo_ref[D_ATT:, :], preferred_element_type=f32))
    h1_ref[...] = h1
    hn = h1 * lax.rsqrt(jnp.mean(h1 * h1, axis=-1, keepdims=True) + RMS_EPS) * g_ref[...]

    hn_hi = hn.astype(bf16)
    hn_lo = (hn - hn_hi.astype(f32)).astype(bf16)
    nt = (((1,), (1,)), ((), ()))
    logits = (lax.dot_general(wr_hi_ref[...], hn_hi, nt, preferred_element_type=f32)
              + lax.dot_general(wr_hi_ref[...], hn_lo, nt, preferred_element_type=f32)
              + lax.dot_general(wr_lo_ref[...], hn_hi, nt, preferred_element_type=f32)
              + br_ref[...])

    erow = lax.broadcasted_iota(jnp.int32, logits.shape, 0)
    work = logits
    vals, idxs = [], []
    for _ in range(TOP_K):
        mk = jnp.max(work, axis=0, keepdims=True)
        ik = jnp.min(jnp.where(work == mk, erow, N_EXPERTS), axis=0, keepdims=True)
        work = jnp.where(erow == ik, -jnp.inf, work)
        vals.append(mk)
        idxs.append(ik)
    exps = [jnp.exp(v - vals[0]) for v in vals]
    denom = exps[0] + exps[1] + exps[2] + exps[3]

    onehot = jnp.zeros(logits.shape, f32)
    gates = jnp.zeros(logits.shape, f32)
    for k in range(TOP_K):
        hit = erow == idxs[k]
        onehot = onehot + jnp.where(hit, 1.0, 0.0)
        gates = gates + jnp.where(hit, exps[k] / denom, 0.0)

    before = jnp.dot(onehot.astype(bf16), triu_ref[...], preferred_element_type=f32)
    count = jnp.sum(onehot, axis=1, keepdims=True)
    slots = jnp.floor((count + (SLOT_ROWS - 1)) * (1.0 / SLOT_ROWS))
    slots_b = jnp.broadcast_to(slots, (N_EXPERTS, LANES))
    slot_start = jnp.dot(tril_ref[...], slots_b.astype(bf16), preferred_element_type=f32)
    cnt_ref[0] = slots_b
    pos = before + slot_start[:, 0:1] * SLOT_ROWS
    dls = []
    for k in range(TOP_K):
        dk = jnp.sum(jnp.where(erow == idxs[k], pos, 0.0), axis=0, keepdims=True)
        dl_ref[k:k + 1, :] = dk.astype(jnp.int32)
        dls.append(dk)

    g_hi, g_mid, g_lo = _split3(gates)
    tm = logits.shape[1]
    r8 = lax.broadcasted_iota(jnp.int32, (8, tm), 0)
    dl8 = jnp.zeros((8, tm), f32)
    for k in range(TOP_K):
        dl8 = jnp.where(r8 == k, dls[k], dl8)
    stack = jnp.concatenate(
        [g_hi.astype(f32), g_mid.astype(f32), g_lo.astype(f32), dl8,
         jnp.zeros((LANES - 3 * N_EXPERTS - 8, tm), f32)], axis=0)
    col = stack.T
    col_ref[...] = col
    lane = lax.broadcasted_iota(jnp.int32, col.shape, 1)
    hna_ref[:, :D_MODEL] = hn_hi
    hna_ref[:, D_MODEL:] = jnp.where(lane < 3 * N_EXPERTS, col, 0.0).astype(bf16)


def _outproj_router(att2d, conv2d, x2d, w_out, norm_ffn_g, w_router, b_router):
    n = x2d.shape[0]
    tm = ROW_TILE
    wr_t = w_router.T
    wr_hi = wr_t.astype(bf16)
    wr_lo = (wr_t - wr_hi.astype(f32)).astype(bf16)
    triu = jnp.asarray(np.triu(np.ones((tm, tm), np.float32), k=1), bf16)
    tril = jnp.asarray(np.tril(np.ones((N_EXPERTS, N_EXPERTS), np.float32), k=-1), bf16)
    const = lambda shape: pl.BlockSpec(shape, lambda i: (0,) * len(shape))
    rows = lambda width: pl.BlockSpec((tm, width), lambda i: (i, 0))
    return pl.pallas_call(
        _outproj_router_kernel,
        grid=(n // tm,),
        in_specs=[
            rows(D_ATT), rows(D_CONV), rows(D_MODEL),
            const((D_MODEL, D_MODEL)), const((1, D_MODEL)),
            const((N_EXPERTS, D_MODEL)), const((N_EXPERTS, D_MODEL)), const((N_EXPERTS, 1)),
            const((tm, tm)), const((N_EXPERTS, N_EXPERTS)),
        ],
        out_specs=[
            rows(D_MODEL), rows(X_WIDTH),
            pl.BlockSpec((TOP_K, tm), lambda i: (0, i)),
            rows(LANES),
            pl.BlockSpec((1, N_EXPERTS, LANES), lambda i: (i, 0, 0)),
        ],
        out_shape=[
            jax.ShapeDtypeStruct((n, D_MODEL), f32),
            jax.ShapeDtypeStruct((n, X_WIDTH), bf16),
            jax.ShapeDtypeStruct((TOP_K, n), jnp.int32),
            jax.ShapeDtypeStruct((n, LANES), f32),
            jax.ShapeDtypeStruct((n // tm, N_EXPERTS, LANES), f32),
        ],
        compiler_params=pltpu.CompilerParams(
            dimension_semantics=("arbitrary",), vmem_limit_bytes=VMEM_LIMIT),
        name="outproj_router",
    )(att2d, conv2d, x2d, w_out.astype(bf16), norm_ffn_g.reshape(1, D_MODEL), wr_hi, wr_lo,
      b_router.reshape(N_EXPERTS, 1), triu, tril)


def _dispatch_kernel(pdest_ref, npiece_ref, padd_ref, npad_ref, nused_ref, dl_ref, hna_ref, xs_ref,
                     xbuf, zbuf, sems, zsem, *, n_tiles, n_group_tiles):
    i = pl.program_id(0)
    slot = i % 2

    def piece_copy(t, s, j):
        dst = pl.multiple_of(pdest_ref[t, j], SLOT_ROWS)
        return pltpu.make_async_copy(
            xbuf.at[s, pl.ds(pl.multiple_of(j * SLOT_ROWS, SLOT_ROWS), SLOT_ROWS)],
            xs_ref.at[pl.ds(dst, SLOT_ROWS)], sems.at[s])

    def pad_copy(m):
        dst = pl.multiple_of(padd_ref[m], SLOT_ROWS)
        return pltpu.make_async_copy(
            zbuf.at[pl.ds(0, SLOT_ROWS)], xs_ref.at[pl.ds(dst, SLOT_ROWS)], zsem)

    def tail_copy(t):
        dst = pl.multiple_of(t * GROUP_TILE, GROUP_TILE)
        return pltpu.make_async_copy(zbuf, xs_ref.at[pl.ds(dst, GROUP_TILE)], zsem)

    @pl.when(i >= 2)
    def _():
        def wait(j, c):
            piece_copy(i - 2, slot, j).wait()
            return c
        lax.fori_loop(0, npiece_ref[jnp.maximum(i - 2, 0)], wait, 0)

    @pl.when(i < n_tiles)
    def _():
        dl = dl_ref[...]
        rhs = hna_ref[...]
        for c in range(TILE_SLOT_ROWS // PERM_CHUNK):
            r = lax.broadcasted_iota(jnp.int32, (PERM_CHUNK, dl.shape[1]), 0) + c * PERM_CHUNK
            p = jnp.zeros(r.shape, f32)
            for k in range(TOP_K):
                p = jnp.where(dl[k:k + 1, :] == r, 1.0, p)
            xbuf[slot, c * PERM_CHUNK:(c + 1) * PERM_CHUNK, :] = jnp.dot(
                p.astype(bf16), rhs, preferred_element_type=f32).astype(bf16)

        def start(j, c):
            piece_copy(i, slot, j).start()
            return c
        lax.fori_loop(0, npiece_ref[jnp.minimum(i, n_tiles - 1)], start, 0)

    @pl.when(i == n_tiles)
    def _():
        zbuf[...] = jnp.zeros_like(zbuf)

        def start(m, c):
            pad_copy(m).start()
            return c
        lax.fori_loop(0, npad_ref[0], start, 0)

        def start_tail(t, c):
            tail_copy(t).start()
            return c
        lax.fori_loop(nused_ref[0], n_group_tiles, start_tail, 0)

    @pl.when(i == n_tiles + 1)
    def _():
        def wait(m, c):
            pad_copy(m).wait()
            return c
        lax.fori_loop(0, npad_ref[0], wait, 0)

        def wait_tail(t, c):
            tail_copy(t).wait()
            return c
        lax.fori_loop(nused_ref[0], n_group_tiles, wait_tail, 0)


def _dispatch(hna, dl, plan, n_rows):
    n = hna.shape[0]
    tm = ROW_TILE
    nt = n // tm
    last = nt - 1
    grid_spec = pltpu.PrefetchScalarGridSpec(
        num_scalar_prefetch=5,
        grid=(nt + 2,),
        in_specs=[
            pl.BlockSpec((TOP_K, tm), lambda i, *_: (0, jnp.minimum(i, last))),
            pl.BlockSpec((tm, X_WIDTH), lambda i, *_: (jnp.minimum(i, last), 0)),
        ],
        out_specs=pl.BlockSpec(memory_space=pl.ANY),
        scratch_shapes=[
            pltpu.VMEM((2, TILE_SLOT_ROWS, X_WIDTH), bf16),
            pltpu.VMEM((GROUP_TILE, X_WIDTH), bf16),
            pltpu.SemaphoreType.DMA((2,)),
            pltpu.SemaphoreType.DMA,
        ],
    )
    return pl.pallas_call(
        functools.partial(_dispatch_kernel, n_tiles=nt, n_group_tiles=n_rows // GROUP_TILE),
        grid_spec=grid_spec,
        out_shape=jax.ShapeDtypeStruct((n_rows, X_WIDTH), bf16),
        compiler_params=pltpu.CompilerParams(
            dimension_semantics=("arbitrary",), vmem_limit_bytes=VMEM_LIMIT),
        name="dispatch",
    )(plan["piece_dest"], plan["n_pieces"], plan["pad_dest"], plan["n_pad"], plan["n_used"], dl, hna)


def _moe_kernel(rstart_ref, rtiles_ref, nused_ref, xs_ref, wgu_ref, bgu_ref, wd_ref, bd_ref, ys_ref,
                wgu_bf, wd_bf, xbuf, ybuf, xsem, ysem, *, n_group_tiles):
    e = pl.program_id(0)
    n_t = rtiles_ref[e]
    row0 = rstart_ref[e]

    def x_copy(t, s):
        src = pl.multiple_of(row0 + t * GROUP_TILE, GROUP_TILE)
        return pltpu.make_async_copy(xs_ref.at[pl.ds(src, GROUP_TILE)], xbuf.at[s], xsem.at[s])

    def y_copy(t, s):
        dst = pl.multiple_of(row0 + t * GROUP_TILE, GROUP_TILE)
        return pltpu.make_async_copy(ybuf.at[s], ys_ref.at[pl.ds(dst, GROUP_TILE)], ysem.at[s])

    def tail_copy(t):
        dst = pl.multiple_of(t * GROUP_TILE, GROUP_TILE)
        return pltpu.make_async_copy(ybuf.at[0], ys_ref.at[pl.ds(dst, GROUP_TILE)], ysem.at[0])

    @pl.when(n_t > 0)
    def _():
        x_copy(0, 0).start()
        wgu_bf[...] = wgu_ref[0].astype(bf16)
        wd_bf[...] = wd_ref[0].astype(bf16)

        def body(t, carry):
            s = t % 2

            @pl.when(t + 1 < n_t)
            def _():
                x_copy(t + 1, 1 - s).start()

            x_copy(t, s).wait()

            @pl.when(t >= 2)
            def _():
                y_copy(t - 2, s).wait()

            h = jnp.dot(xbuf[s, :, :D_MODEL], wgu_bf[...], preferred_element_type=f32) + bgu_ref[0]
            hg = jnp.minimum(h[:, :D_FF], SWIGLU_LIMIT)
            hu = jnp.clip(h[:, D_FF:], -SWIGLU_LIMIT, SWIGLU_LIMIT)
            act = (hu + 1.0) * (hg * (1.0 / (1.0 + jnp.exp(-SWIGLU_ALPHA * hg))))
            y = jnp.dot(act.astype(bf16), wd_bf[...], preferred_element_type=f32) + bd_ref[0]
            aug = xbuf[s, :, D_MODEL:].astype(f32)
            lane = lax.broadcasted_iota(jnp.int32, aug.shape, 1)
            gate = jnp.sum(jnp.where((lane & (N_EXPERTS - 1)) == e, aug, 0.0), axis=1, keepdims=True)
            ybuf[s] = (gate * y).astype(ybuf.dtype)
            y_copy(t, s).start()
            return carry

        lax.fori_loop(0, n_t, body, 0)

        @pl.when(n_t >= 2)
        def _():
            y_copy(n_t - 2, n_t % 2).wait()

        y_copy(n_t - 1, (n_t - 1) % 2).wait()

    @pl.when(e == N_EXPERTS - 1)
    def _():
        ybuf[0] = jnp.zeros(ybuf.shape[1:], ybuf.dtype)

        def start_tail(t, carry):
            tail_copy(t).start()
            return carry

        def wait_tail(t, carry):
            tail_copy(t).wait()
            return carry

        lax.fori_loop(nused_ref[0], n_group_tiles, start_tail, 0)
        lax.fori_loop(nused_ref[0], n_group_tiles, wait_tail, 0)


def _moe(xs, plan, w_gu, b_gu, w_down, b_down):
    n_rows = xs.shape[0]
    tg = GROUP_TILE
    grid_spec = pltpu.PrefetchScalarGridSpec(
        num_scalar_prefetch=3,
        grid=(N_EXPERTS,),
        in_specs=[
            pl.BlockSpec(memory_space=pl.ANY),
            pl.BlockSpec((1, D_MODEL, 2 * D_FF), lambda e, *_: (e, 0, 0)),
            pl.BlockSpec((1, 1, 2 * D_FF), lambda e, *_: (e, 0, 0)),
            pl.BlockSpec((1, D_FF, D_MODEL), lambda e, *_: (e, 0, 0)),
            pl.BlockSpec((1, 1, D_MODEL), lambda e, *_: (e, 0, 0)),
        ],
        out_specs=pl.BlockSpec(memory_space=pl.ANY),
        scratch_shapes=[
            pltpu.VMEM((D_MODEL, 2 * D_FF), bf16),
            pltpu.VMEM((D_FF, D_MODEL), bf16),
            pltpu.VMEM((2, tg, X_WIDTH), bf16),
            pltpu.VMEM((2, tg, D_MODEL), bf16),
            pltpu.SemaphoreType.DMA((2,)),
            pltpu.SemaphoreType.DMA((2,)),
        ],
    )
    return pl.pallas_call(
        functools.partial(_moe_kernel, n_group_tiles=n_rows // tg),
        grid_spec=grid_spec,
        out_shape=jax.ShapeDtypeStruct((n_rows, D_MODEL), bf16),
        compiler_params=pltpu.CompilerParams(
            dimension_semantics=("arbitrary",), vmem_limit_bytes=VMEM_LIMIT),
        name="moe_experts",
    )(plan["region_start"], plan["region_tiles"], plan["n_used"], xs, w_gu,
      b_gu.reshape(N_EXPERTS, 1, 2 * D_FF), w_down, b_down.reshape(N_EXPERTS, 1, D_MODEL))


def _combine_kernel(pdest_ref, npiece_ref, h1_ref, col_ref, g_ref, ys_ref, o_ref, ybuf, sems,
                    *, n_tiles):
    i = pl.program_id(0)
    slot = i % 2

    def piece_copy(t, s, j):
        src = pl.multiple_of(pdest_ref[t, j], SLOT_ROWS)
        return pltpu.make_async_copy(
            ys_ref.at[pl.ds(src, SLOT_ROWS)],
            ybuf.at[s, pl.ds(pl.multiple_of(j * SLOT_ROWS, SLOT_ROWS), SLOT_ROWS)], sems.at[s])

    def fetch(t, s):
        def start(j, c):
            piece_copy(t, s, j).start()
            return c
        lax.fori_loop(0, npiece_ref[t], start, 0)

    @pl.when(i == 0)
    def _():
        ybuf[...] = jnp.zeros_like(ybuf)
        fetch(0, 0)

    @pl.when(i + 1 < n_tiles)
    def _():
        fetch(jnp.minimum(i + 1, n_tiles - 1), 1 - slot)

    def wait(j, c):
        piece_copy(i, slot, j).wait()
        return c
    lax.fori_loop(0, npiece_ref[i], wait, 0)

    col = col_ref[...]
    rows = [col[:, 3 * N_EXPERTS + k:3 * N_EXPERTS + k + 1].astype(jnp.int32) for k in range(TOP_K)]
    h = h1_ref[...]
    for c in range(TILE_SLOT_ROWS // PERM_CHUNK):
        r = lax.broadcasted_iota(jnp.int32, (col.shape[0], PERM_CHUNK), 1) + c * PERM_CHUNK
        g = jnp.zeros(r.shape, f32)
        for k in range(TOP_K):
            g = jnp.where(rows[k] == r, 1.0, g)
        h = h + jnp.dot(g.astype(bf16), ybuf[slot, c * PERM_CHUNK:(c + 1) * PERM_CHUNK, :],
                        preferred_element_type=f32)
    o_ref[...] = h * lax.rsqrt(jnp.mean(h * h, axis=-1, keepdims=True) + RMS_EPS) * g_ref[...]


def _combine(h1, ys, col, plan, norm_final_g):
    n = h1.shape[0]
    tm = ROW_TILE
    nt = n // tm
    grid_spec = pltpu.PrefetchScalarGridSpec(
        num_scalar_prefetch=2,
        grid=(nt,),
        in_specs=[
            pl.BlockSpec((tm, D_MODEL), lambda i, *_: (i, 0)),
            pl.BlockSpec((tm, LANES), lambda i, *_: (i, 0)),
            pl.BlockSpec((1, D_MODEL), lambda i, *_: (0, 0)),
            pl.BlockSpec(memory_space=pl.ANY),
        ],
        out_specs=pl.BlockSpec((tm, D_MODEL), lambda i, *_: (i, 0)),
        scratch_shapes=[
            pltpu.VMEM((2, TILE_SLOT_ROWS, D_MODEL), bf16),
            pltpu.SemaphoreType.DMA((2,)),
        ],
    )
    return pl.pallas_call(
        functools.partial(_combine_kernel, n_tiles=nt),
        grid_spec=grid_spec,
        out_shape=jax.ShapeDtypeStruct((n, D_MODEL), f32),
        compiler_params=pltpu.CompilerParams(
            dimension_semantics=("arbitrary",), vmem_limit_bytes=VMEM_LIMIT),
        name="combine",
    )(plan["piece_dest"], plan["n_pieces"], h1, col, norm_final_g.reshape(1, D_MODEL), ys)


def _routing_plan(slot_counts):
    tg = GROUP_TILE
    i32 = jnp.int32
    pc = slot_counts.astype(i32) * SLOT_ROWS
    local_end = jnp.cumsum(pc, axis=1)
    local_start = local_end - pc
    total = jnp.sum(pc, axis=0)
    region = ((total + tg - 1) // tg) * tg
    region_end = jnp.cumsum(region)
    region_start = region_end - region
    base = region_start[None, :] + jnp.cumsum(pc, axis=0) - pc
    n_pieces = local_end[:, -1] // SLOT_ROWS

    piece_row = jnp.arange(PIECES_PER_TILE, dtype=i32) * SLOT_ROWS
    owner = jnp.sum((local_end[:, None, :] <= piece_row[None, :, None]).astype(i32), axis=2)
    owner_hit = owner[:, :, None] == jnp.arange(N_EXPERTS, dtype=i32)[None, None, :]
    shift = jnp.sum(jnp.where(owner_hit, (base - local_start)[:, None, :], 0), axis=2)
    piece_dest = shift + piece_row[None, :]
    piece_dest = jnp.where(piece_row[None, :] < local_end[:, -1:], piece_dest, 0)

    pad_slots = (region - total) // SLOT_ROWS
    pad_end = jnp.cumsum(pad_slots)
    m = jnp.arange(N_EXPERTS * (tg // SLOT_ROWS), dtype=i32)
    pad_owner = jnp.minimum(jnp.sum((pad_end[None, :] <= m[:, None]).astype(i32), axis=1),
                            N_EXPERTS - 1)
    pad_hit = pad_owner[:, None] == jnp.arange(N_EXPERTS, dtype=i32)[None, :]
    pad_first = jnp.sum(jnp.where(pad_hit, (region_start + total - (pad_end - pad_slots) * SLOT_ROWS)
                                  [None, :], 0), axis=1)
    pad_dest = jnp.where(m < pad_end[-1], pad_first + m * SLOT_ROWS, 0)

    return {
        "piece_dest": piece_dest, "n_pieces": n_pieces, "pad_dest": pad_dest, "n_pad": pad_end[-1:],
        "region_start": region_start, "region_tiles": region // tg,
        "n_used": (region_end[-1] // tg).reshape(1),
    }


def kernel(x, norm_mix_g, w_in, b_f, b_glu, w_dw, b_dw, ln_g, ln_b, w_out, norm_ffn_g, w_router,
           b_router, w_gu, b_gu, w_down, b_down, norm_final_g):
    batch, seq, d = x.shape
    n = batch * seq
    x2d = x.reshape(n, d)

    qa, kta, v, u = _inproj(x2d, norm_mix_g, w_in, b_f, b_glu, batch=batch, seq=seq)
    att = _attention(qa, kta, v.reshape(batch, seq, D_ATT), batch=batch, seq=seq)
    conv = _conv_module(u.reshape(batch, seq, D_CONV), w_dw, b_dw, ln_g, ln_b, batch=batch, seq=seq)

    h1, hna, dl, col, slot_counts = _outproj_router(
        att.reshape(n, D_ATT), conv.reshape(n, D_CONV), x2d, w_out, norm_ffn_g, w_router, b_router)

    n_token_tiles = n // ROW_TILE
    max_rows = n * TOP_K + N_EXPERTS * (n_token_tiles * (SLOT_ROWS - 1) + GROUP_TILE - 1)
    n_group_tiles = -(-max_rows // GROUP_TILE)
    plan = _routing_plan(slot_counts[:, :, 0])
    xs = _dispatch(hna, dl, plan, n_group_tiles * GROUP_TILE)
    ys = _moe(xs, plan, w_gu, b_gu, w_down, b_down)
    out = _combine(h1, ys, col, plan, norm_final_g)
    return out.reshape(batch, seq, d)
```

```python
import functools
import math

import numpy as np
import jax
import jax.numpy as jnp
from jax import lax
from jax.experimental import pallas as pl
from jax.experimental.pallas import tpu as pltpu

D_MODEL = 1024
ATT_HEADS = 8
HEAD_DIM = 64
D_ATT = ATT_HEADS * HEAD_DIM
D_CONV = D_MODEL - D_ATT
CONV_WIDTH = 31
N_EXPERTS = 32
TOP_K = 4
D_FF = 1024
SWIGLU_LIMIT = 7.0
SWIGLU_ALPHA = 1.702
RMS_EPS = 1e-6
LN_EPS = 1e-5
MASK_VALUE = -1e30

LANES = 128
SUBLANES = 8
CONV_ROWS = 64
ROW_TILE = 512
ATT_SUB = 256
ATT_Q_TILES = 2
CONV_HALO = 32
GROUP_TILE = 256
SLOT_ROWS = 16
X_WIDTH = D_MODEL + LANES
TILE_SLOT_ROWS = ROW_TILE * TOP_K + N_EXPERTS * SLOT_ROWS
PIECES_PER_TILE = TILE_SLOT_ROWS // SLOT_ROWS
PERM_CHUNK = 512
VMEM_LIMIT = 56 * 1024 * 1024

AUG_Q = HEAD_DIM
AUG_K = HEAD_DIM + 3
ONES_LANE = ATT_HEADS

f32 = jnp.float32
bf16 = jnp.bfloat16


def _split3(x):
    hi = x.astype(bf16)
    r1 = x - hi.astype(f32)
    mid = r1.astype(bf16)
    lo = (r1 - mid.astype(f32)).astype(bf16)
    return hi, mid, lo


def _selection_matrices():
    sel_q = np.zeros((3, LANES, ATT_HEADS * LANES), np.float32)
    sel_k = np.zeros((3, ATT_HEADS * LANES, LANES), np.float32)
    for h in range(ATT_HEADS):
        base = h * LANES
        for p in range(3):
            sel_q[p, h, base + AUG_Q + p] = 1.0
            sel_q[0, ONES_LANE, base + AUG_K + p] = 1.0
            sel_k[p, base + AUG_K + p, h] = -1.0
            sel_k[0, base + AUG_Q + p, ONES_LANE] = 1.0
    return jnp.asarray(sel_q, bf16), jnp.asarray(sel_k, bf16)


def _inproj_kernel(x_ref, g_ref, wq_ref, wkt_ref, wv_ref, wf_ref, wglu_ref, bf_ref, bglu_ref,
                   selq_ref, selk_ref, tri_ref,
                   qa_ref, kta_ref, v_ref, u_ref, carry_ref, *, tiles_per_seq):
    i = pl.program_id(0)

    @pl.when(i % tiles_per_seq == 0)
    def _():
        carry_ref[...] = jnp.zeros_like(carry_ref)

    x = x_ref[...]
    xn = x * lax.rsqrt(jnp.mean(x * x, axis=-1, keepdims=True) + RMS_EPS) * g_ref[...]
    xb = xn.astype(bf16)

    f = jnp.dot(xb, wf_ref[...], preferred_element_type=f32) + bf_ref[...]
    log_f = jnp.minimum(f, 0.0) - jnp.log1p(jnp.exp(-jnp.abs(f)))
    lane = lax.broadcasted_iota(jnp.int32, log_f.shape, 1)
    log_f = jnp.where(lane < ATT_HEADS, log_f, 0.0)
    tri = tri_ref[...]
    hi, mid, lo = _split3(log_f)
    c = (jnp.dot(tri, hi, preferred_element_type=f32)
         + jnp.dot(tri, mid, preferred_element_type=f32)
         + jnp.dot(tri, lo, preferred_element_type=f32)) + carry_ref[...]
    tm = c.shape[0]
    carry_ref[...] = c[tm - 1:tm, :]

    c_hi, c_mid, c_lo = _split3(c)
    c_hi = jnp.where(lane == ONES_LANE, jnp.ones_like(c_hi), c_hi)
    qa = jnp.dot(xb, wq_ref[...], preferred_element_type=f32) * (1.0 / math.sqrt(HEAD_DIM))
    qa = (qa + jnp.dot(c_hi, selq_ref[0], preferred_element_type=f32)
          + jnp.dot(c_mid, selq_ref[1], preferred_element_type=f32)
          + jnp.dot(c_lo, selq_ref[2], preferred_element_type=f32)).astype(bf16)
    for h in range(ATT_HEADS):
        qa_ref[0, h] = qa[:, h * LANES:(h + 1) * LANES]

    ct = c.T
    ct_hi, ct_mid, ct_lo = _split3(ct)
    row = lax.broadcasted_iota(jnp.int32, ct.shape, 0)
    ct_hi = jnp.where(row == ONES_LANE, jnp.ones_like(ct_hi), ct_hi)
    kta = lax.dot_general(wkt_ref[...], xb, (((1,), (1,)), ((), ())), preferred_element_type=f32)
    kta = (kta + jnp.dot(selk_ref[0], ct_hi, preferred_element_type=f32)
           + jnp.dot(selk_ref[1], ct_mid, preferred_element_type=f32)
           + jnp.dot(selk_ref[2], ct_lo, preferred_element_type=f32)).astype(bf16)
    for h in range(ATT_HEADS):
        kta_ref[0, h, 0] = kta[h * LANES:(h + 1) * LANES, :]

    v_ref[...] = jnp.dot(xb, wv_ref[...], preferred_element_type=f32).astype(bf16)

    glu = jnp.dot(xb, wglu_ref[...], preferred_element_type=f32) + bglu_ref[...]
    a = glu[:, :D_CONV]
    gate = glu[:, D_CONV:]
    u_ref[...] = a * (1.0 / (1.0 + jnp.exp(-gate)))


def _inproj(x2d, norm_mix_g, w_in, b_f, b_glu, *, batch, seq):
    n = batch * seq
    tm = ROW_TILE
    nt = seq // tm
    wq = w_in[:, :D_ATT]
    wk = w_in[:, D_ATT:2 * D_ATT]
    wv = w_in[:, 2 * D_ATT:3 * D_ATT]
    wf = w_in[:, 3 * D_ATT:3 * D_ATT + ATT_HEADS]
    wglu = w_in[:, 3 * D_ATT + ATT_HEADS:]
    pad = LANES - HEAD_DIM
    wq_pad = jnp.pad(wq.reshape(D_MODEL, ATT_HEADS, HEAD_DIM), ((0, 0), (0, 0), (0, pad)))
    wq_pad = wq_pad.reshape(D_MODEL, ATT_HEADS * LANES).astype(bf16)
    wkt_pad = jnp.pad(wk.T.reshape(ATT_HEADS, HEAD_DIM, D_MODEL), ((0, 0), (0, pad), (0, 0)))
    wkt_pad = wkt_pad.reshape(ATT_HEADS * LANES, D_MODEL).astype(bf16)
    wf_pad = jnp.pad(wf, ((0, 0), (0, LANES - ATT_HEADS))).astype(bf16)
    bf_pad = jnp.pad(b_f, (0, LANES - ATT_HEADS)).reshape(1, LANES)
    sel_q, sel_k = _selection_matrices()
    tri = jnp.asarray(np.tril(np.ones((tm, tm), np.float32)), bf16)

    const = lambda shape: pl.BlockSpec(shape, lambda i: (0,) * len(shape))
    return pl.pallas_call(
        functools.partial(_inproj_kernel, tiles_per_seq=nt),
        grid=(n // tm,),
        in_specs=[
            pl.BlockSpec((tm, D_MODEL), lambda i: (i, 0)),
            const((1, D_MODEL)),
            const((D_MODEL, ATT_HEADS * LANES)),
            const((ATT_HEADS * LANES, D_MODEL)),
            const((D_MODEL, D_ATT)),
            const((D_MODEL, LANES)),
            const((D_MODEL, 2 * D_CONV)),
            const((1, LANES)),
            const((1, 2 * D_CONV)),
            const((3, LANES, ATT_HEADS * LANES)),
            const((3, ATT_HEADS * LANES, LANES)),
            const((tm, tm)),
        ],
        out_specs=[
            pl.BlockSpec((1, ATT_HEADS, tm, LANES), lambda i: (i // nt, 0, i % nt, 0)),
            pl.BlockSpec((1, ATT_HEADS, 1, LANES, tm), lambda i: (i // nt, 0, i % nt, 0, 0)),
            pl.BlockSpec((tm, D_ATT), lambda i: (i, 0)),
            pl.BlockSpec((tm, D_CONV), lambda i: (i, 0)),
        ],
        out_shape=[
            jax.ShapeDtypeStruct((batch, ATT_HEADS, seq, LANES), bf16),
            jax.ShapeDtypeStruct((batch, ATT_HEADS, nt, LANES, tm), bf16),
            jax.ShapeDtypeStruct((n, D_ATT), bf16),
            jax.ShapeDtypeStruct((n, D_CONV), f32),
        ],
        scratch_shapes=[pltpu.VMEM((1, LANES), f32)],
        compiler_params=pltpu.CompilerParams(
            dimension_semantics=("arbitrary",), vmem_limit_bytes=VMEM_LIMIT),
        name="inproj",
    )(x2d, norm_mix_g.reshape(1, D_MODEL), wq_pad, wkt_pad, wv.astype(bf16), wf_pad,
      wglu.astype(bf16), bf_pad, b_glu.reshape(1, 2 * D_CONV), sel_q, sel_k, tri)


def _attn_kernel(qa_ref, kta_ref, v_ref, o_ref, acc_ref, m_ref, *, tile):
    i = pl.program_id(2)
    q_rows = ATT_Q_TILES * tile
    first_half = lax.broadcasted_iota(jnp.int32, (tile, LANES), 1) < HEAD_DIM
    acc_ref[...] = jnp.zeros_like(acc_ref)
    m_ref[...] = jnp.full_like(m_ref, MASK_VALUE)
    sub = ATT_SUB
    blocks_per_tile = tile // sub

    def process(kv_tiles):
        v_augs = []
        for j, _ in kv_tiles:
            vp = v_ref[0, pl.ds(pl.multiple_of(j * tile, tile), tile), :]
            one = jnp.ones_like(vp)
            v_augs.append((jnp.where(first_half, vp, one), jnp.where(first_half, one, vp)))
        chains = [(t, h, rb) for t in range(len(kv_tiles)) for h in range(2)
                  for rb in range(q_rows // sub) if kv_tiles[t][1][rb // blocks_per_tile] is not None]

        def visible_width(t, rb):
            if kv_tiles[t][1][rb // blocks_per_tile] == "diag":
                return (rb % blocks_per_tile + 1) * sub
            return tile

        def scores(t, h, rb):
            return jnp.dot(qa_ref[0, h, rb * sub:(rb + 1) * sub, :],
                           kta_ref[0, h, kv_tiles[t][0]][:, :visible_width(t, rb)],
                           preferred_element_type=f32)

        s_next = scores(*chains[0])
        for ci, (t, h, rb) in enumerate(chains):
            s = s_next
            if ci + 1 < len(chains):
                s_next = scores(*chains[ci + 1])
            rows = slice(rb * sub, (rb + 1) * sub)
            width = s.shape[1]
            if kv_tiles[t][1][rb // blocks_per_tile] == "diag":
                r = lax.broadcasted_iota(jnp.int32, s.shape, 0) + (rb % blocks_per_tile) * sub
                c = lax.broadcasted_iota(jnp.int32, s.shape, 1)
                s = jnp.where(c <= r, s, MASK_VALUE)
            m_old = m_ref[h, rows, :]
            m_new = jnp.maximum(m_old, jnp.max(s, axis=-1, keepdims=True))
            alpha = jnp.exp(m_old - m_new)
            p = jnp.exp(s - jnp.concatenate([m_new] * (width // LANES), axis=1)).astype(bf16)
            acc_ref[h, rows, :] = alpha * acc_ref[h, rows, :] + jnp.dot(
                p, v_augs[t][h][:width, :], preferred_element_type=f32)
            m_ref[h, rows, :] = m_new

    everything = ("full",) * ATT_Q_TILES

    def body(jj, carry):
        process([(ATT_Q_TILES * jj + t, everything) for t in range(ATT_Q_TILES)])
        return carry

    lax.fori_loop(0, i, body, 0)
    process([(ATT_Q_TILES * i + t,
              tuple("full" if t < qt else ("diag" if t == qt else None) for qt in range(ATT_Q_TILES)))
             for t in range(ATT_Q_TILES)])

    for qt in range(ATT_Q_TILES):
        a0 = acc_ref[0, qt * tile:(qt + 1) * tile, :]
        a1 = acc_ref[1, qt * tile:(qt + 1) * tile, :]
        l0 = a0[:, HEAD_DIM:HEAD_DIM + 1]
        l1 = a1[:, 0:1]
        o_ref[0, qt * tile:(qt + 1) * tile, :] = jnp.where(
            first_half, a0 / l0, a1 / l1).astype(o_ref.dtype)


def _attention(qa, kta, v3, *, batch, seq):
    tile = ROW_TILE
    nt = seq // tile
    q_rows = ATT_Q_TILES * tile
    return pl.pallas_call(
        functools.partial(_attn_kernel, tile=tile),
        grid=(batch, ATT_HEADS // 2, seq // q_rows),
        in_specs=[
            pl.BlockSpec((1, 2, q_rows, LANES), lambda b, hp, i: (b, hp, i, 0)),
            pl.BlockSpec((1, 2, nt, LANES, tile), lambda b, hp, i: (b, hp, 0, 0, 0)),
            pl.BlockSpec((1, seq, LANES), lambda b, hp, i: (b, 0, hp)),
        ],
        out_specs=pl.BlockSpec((1, q_rows, LANES), lambda b, hp, i: (b, i, hp)),
        out_shape=jax.ShapeDtypeStruct((batch, seq, D_ATT), bf16),
        scratch_shapes=[pltpu.VMEM((2, q_rows, LANES), f32), pltpu.VMEM((2, q_rows, LANES), f32)],
        compiler_params=pltpu.CompilerParams(
            dimension_semantics=("arbitrary", "arbitrary", "arbitrary"),
            vmem_limit_bytes=VMEM_LIMIT),
        name="attention",
    )(qa, kta, v3)


def _conv_kernel(prev_ref, cur_ref, w_ref, b_ref, g_ref, beta_ref, o_ref, ext_ref, *, tile):
    i = pl.program_id(1)
    prev = prev_ref[0]
    ext_ref[0, 0:CONV_HALO, :] = jnp.where(i == 0, jnp.zeros_like(prev), prev)
    ext_ref[0, CONV_HALO:, :] = cur_ref[0]
    span = tile + CONV_HALO - SUBLANES
    for s in range(1, SUBLANES):
        ext_ref[s, 0:span, :] = ext_ref[0, s:s + span, :]
    shift = CONV_HALO - (CONV_WIDTH - 1)

    def rows_block(cb, carry):
        r0 = pl.multiple_of(cb * CONV_ROWS, CONV_ROWS)
        acc = jnp.zeros((CONV_ROWS, D_CONV), f32) + b_ref[...]
        for j in range(CONV_WIDTH):
            s = (shift + j) % SUBLANES
            a = shift + j - s
            acc = acc + w_ref[j:j + 1, :] * ext_ref[s, pl.ds(r0 + a, CONV_ROWS), :]
        mu = jnp.mean(acc, axis=-1, keepdims=True)
        d = acc - mu
        var = jnp.mean(d * d, axis=-1, keepdims=True)
        y = d * lax.rsqrt(var + LN_EPS) * g_ref[...] + beta_ref[...]
        o_ref[0, pl.ds(r0, CONV_ROWS), :] = (y * (1.0 / (1.0 + jnp.exp(-y)))).astype(o_ref.dtype)
        return carry

    lax.fori_loop(0, tile // CONV_ROWS, rows_block, 0)


def _conv_module(u3, w_dw, b_dw, ln_g, ln_b, *, batch, seq):
    tile = ROW_TILE
    halo_blocks = tile // CONV_HALO
    w_pad = jnp.pad(w_dw, ((0, CONV_HALO - CONV_WIDTH), (0, 0)))
    vec = lambda: pl.BlockSpec((1, D_CONV), lambda b, i: (0, 0))
    return pl.pallas_call(
        functools.partial(_conv_kernel, tile=tile),
        grid=(batch, seq // tile),
        in_specs=[
            pl.BlockSpec((1, CONV_HALO, D_CONV),
                         lambda b, i: (b, jnp.maximum(i * halo_blocks - 1, 0), 0)),
            pl.BlockSpec((1, tile, D_CONV), lambda b, i: (b, i, 0)),
            pl.BlockSpec((CONV_HALO, D_CONV), lambda b, i: (0, 0)),
            vec(), vec(), vec(),
        ],
        out_specs=pl.BlockSpec((1, tile, D_CONV), lambda b, i: (b, i, 0)),
        out_shape=jax.ShapeDtypeStruct((batch, seq, D_CONV), bf16),
        scratch_shapes=[pltpu.VMEM((SUBLANES, tile + CONV_HALO, D_CONV), f32)],
        compiler_params=pltpu.CompilerParams(
            dimension_semantics=("arbitrary", "arbitrary"), vmem_limit_bytes=VMEM_LIMIT),
        name="conv_module",
    )(u3, u3, w_pad, b_dw.reshape(1, D_CONV), ln_g.reshape(1, D_CONV), ln_b.reshape(1, D_CONV))


def _outproj_router_kernel(att_ref, conv_ref, x_ref, wo_ref, g_ref, wr_hi_ref, wr_lo_ref, br_ref,
                           triu_ref, tril_ref, h1_ref, hna_ref, dl_ref, col_ref, cnt_ref):
    h1 = (x_ref[...]
          + jnp.dot(att_ref[...], wo_ref[:D_ATT, :], preferred_element_type=f32)
          + jnp.dot(conv_ref[...], wo_ref[D_ATT:, :], preferred_element_type=f32))
    h1_ref[...] = h1
    hn = h1 * lax.rsqrt(jnp.mean(h1 * h1, axis=-1, keepdims=True) + RMS_EPS) * g_ref[...]

    hn_hi = hn.astype(bf16)
    hn_lo = (hn - hn_hi.astype(f32)).astype(bf16)
    nt = (((1,), (1,)), ((), ()))
    logits = (lax.dot_general(wr_hi_ref[...], hn_hi, nt, preferred_element_type=f32)
              + lax.dot_general(wr_hi_ref[...], hn_lo, nt, preferred_element_type=f32)
              + lax.dot_general(wr_lo_ref[...], hn_hi, nt, preferred_element_type=f32)
              + br_ref[...])

    erow = lax.broadcasted_iota(jnp.int32, logits.shape, 0)
    work = logits
    vals, idxs = [], []
    for _ in range(TOP_K):
        mk = jnp.max(work, axis=0, keepdims=True)
        ik = jnp.min(jnp.where(work == mk, erow, N_EXPERTS), axis=0, keepdims=True)
        work = jnp.where(erow == ik, -jnp.inf, work)
        vals.append(mk)
        idxs.append(ik)
    exps = [jnp.exp(v - vals[0]) for v in vals]
    denom = exps[0] + exps[1] + exps[2] + exps[3]

    onehot = jnp.zeros(logits.shape, f32)
    gates = jnp.zeros(logits.shape, f32)
    for k in range(TOP_K):
        hit = erow == idxs[k]
        onehot = onehot + jnp.where(hit, 1.0, 0.0)
        gates = gates + jnp.where(hit, exps[k] / denom, 0.0)

    before = jnp.dot(onehot.astype(bf16), triu_ref[...], preferred_element_type=f32)
    count = jnp.sum(onehot, axis=1, keepdims=True)
    slots = jnp.floor((count + (SLOT_ROWS - 1)) * (1.0 / SLOT_ROWS))
    slots_b = jnp.broadcast_to(slots, (N_EXPERTS, LANES))
    slot_start = jnp.dot(tril_ref[...], slots_b.astype(bf16), preferred_element_type=f32)
    cnt_ref[0] = slots_b
    pos = before + slot_start[:, 0:1] * SLOT_ROWS
    dls = []
    for k in range(TOP_K):
        dk = jnp.sum(jnp.where(erow == idxs[k], pos, 0.0), axis=0, keepdims=True)
        dl_ref[k:k + 1, :] = dk.astype(jnp.int32)
        dls.append(dk)

    g_hi, g_mid, g_lo = _split3(gates)
    tm = logits.shape[1]
    r8 = lax.broadcasted_iota(jnp.int32, (8, tm), 0)
    dl8 = jnp.zeros((8, tm), f32)
    for k in range(TOP_K):
        dl8 = jnp.where(r8 == k, dls[k], dl8)
    stack = jnp.concatenate(
        [g_hi.astype(f32), g_mid.astype(f32), g_lo.astype(f32), dl8,
         jnp.zeros((LANES - 3 * N_EXPERTS - 8, tm), f32)], axis=0)
    col = stack.T
    col_ref[...] = col
    lane = lax.broadcasted_iota(jnp.int32, col.shape, 1)
    hna_ref[:, :D_MODEL] = hn_hi
    hna_ref[:, D_MODEL:] = jnp.where(lane < 3 * N_EXPERTS, col, 0.0).astype(bf16)


def _outproj_router(att2d, conv2d, x2d, w_out, norm_ffn_g, w_router, b_router):
    n = x2d.shape[0]
    tm = ROW_TILE
    wr_t = w_router.T
    wr_hi = wr_t.astype(bf16)
    wr_lo = (wr_t - wr_hi.astype(f32)).astype(bf16)
    triu = jnp.asarray(np.triu(np.ones((tm, tm), np.float32), k=1), bf16)
    tril = jnp.asarray(np.tril(np.ones((N_EXPERTS, N_EXPERTS), np.float32), k=-1), bf16)
    const = lambda shape: pl.BlockSpec(shape, lambda i: (0,) * len(shape))
    rows = lambda width: pl.BlockSpec((tm, width), lambda i: (i, 0))
    return pl.pallas_call(
        _outproj_router_kernel,
        grid=(n // tm,),
        in_specs=[
            rows(D_ATT), rows(D_CONV), rows(D_MODEL),
            const((D_MODEL, D_MODEL)), const((1, D_MODEL)),
            const((N_EXPERTS, D_MODEL)), const((N_EXPERTS, D_MODEL)), const((N_EXPERTS, 1)),
            const((tm, tm)), const((N_EXPERTS, N_EXPERTS)),
        ],
        out_specs=[
            rows(D_MODEL), rows(X_WIDTH),
            pl.BlockSpec((TOP_K, tm), lambda i: (0, i)),
            rows(LANES),
            pl.BlockSpec((1, N_EXPERTS, LANES), lambda i: (i, 0, 0)),
        ],
        out_shape=[
            jax.ShapeDtypeStruct((n, D_MODEL), f32),
            jax.ShapeDtypeStruct((n, X_WIDTH), bf16),
            jax.ShapeDtypeStruct((TOP_K, n), jnp.int32),
            jax.ShapeDtypeStruct((n, LANES), f32),
            jax.ShapeDtypeStruct((n // tm, N_EXPERTS, LANES), f32),
        ],
        compiler_params=pltpu.CompilerParams(
            dimension_semantics=("arbitrary",), vmem_limit_bytes=VMEM_LIMIT),
        name="outproj_router",
    )(att2d, conv2d, x2d, w_out.astype(bf16), norm_ffn_g.reshape(1, D_MODEL), wr_hi, wr_lo,
      b_router.reshape(N_EXPERTS, 1), triu, tril)


def _dispatch_kernel(pdest_ref, npiece_ref, padd_ref, npad_ref, nused_ref, dl_ref, hna_ref, xs_ref,
                     xbuf, zbuf, sems, zsem, *, n_tiles, n_group_tiles):
    i = pl.program_id(0)
    slot = i % 2

    def piece_copy(t, s, j):
        dst = pl.multiple_of(pdest_ref[t, j], SLOT_ROWS)
        return pltpu.make_async_copy(
            xbuf.at[s, pl.ds(pl.multiple_of(j * SLOT_ROWS, SLOT_ROWS), SLOT_ROWS)],
            xs_ref.at[pl.ds(dst, SLOT_ROWS)], sems.at[s])

    def pad_copy(m):
        dst = pl.multiple_of(padd_ref[m], SLOT_ROWS)
        return pltpu.make_async_copy(
            zbuf.at[pl.ds(0, SLOT_ROWS)], xs_ref.at[pl.ds(dst, SLOT_ROWS)], zsem)

    def tail_copy(t):
        dst = pl.multiple_of(t * GROUP_TILE, GROUP_TILE)
        return pltpu.make_async_copy(zbuf, xs_ref.at[pl.ds(dst, GROUP_TILE)], zsem)

    @pl.when(i >= 2)
    def _():
        def wait(j, c):
            piece_copy(i - 2, slot, j).wait()
            return c
        lax.fori_loop(0, npiece_ref[jnp.maximum(i - 2, 0)], wait, 0)

    @pl.when(i < n_tiles)
    def _():
        dl = dl_ref[...]
        rhs = hna_ref[...]
        for c in range(TILE_SLOT_ROWS // PERM_CHUNK):
            r = lax.broadcasted_iota(jnp.int32, (PERM_CHUNK, dl.shape[1]), 0) + c * PERM_CHUNK
            p = jnp.zeros(r.shape, f32)
            for k in range(TOP_K):
                p = jnp.where(dl[k:k + 1, :] == r, 1.0, p)
            xbuf[slot, c * PERM_CHUNK:(c + 1) * PERM_CHUNK, :] = jnp.dot(
                p.astype(bf16), rhs, preferred_element_type=f32).astype(bf16)

        def start(j, c):
            piece_copy(i, slot, j).start()
            return c
        lax.fori_loop(0, npiece_ref[jnp.minimum(i, n_tiles - 1)], start, 0)

    @pl.when(i == n_tiles)
    def _():
        zbuf[...] = jnp.zeros_like(zbuf)

        def start(m, c):
            pad_copy(m).start()
            return c
        lax.fori_loop(0, npad_ref[0], start, 0)

        def start_tail(t, c):
            tail_copy(t).start()
            return c
        lax.fori_loop(nused_ref[0], n_group_tiles, start_tail, 0)

    @pl.when(i == n_tiles + 1)
    def _():
        def wait(m, c):
            pad_copy(m).wait()
            return c
        lax.fori_loop(0, npad_ref[0], wait, 0)

        def wait_tail(t, c):
            tail_copy(t).wait()
            return c
        lax.fori_loop(nused_ref[0], n_group_tiles, wait_tail, 0)


def _dispatch(hna, dl, plan, n_rows):
    n = hna.shape[0]
    tm = ROW_TILE
    nt = n // tm
    last = nt - 1
    grid_spec = pltpu.PrefetchScalarGridSpec(
        num_scalar_prefetch=5,
        grid=(nt + 2,),
        in_specs=[
            pl.BlockSpec((TOP_K, tm), lambda i, *_: (0, jnp.minimum(i, last))),
            pl.BlockSpec((tm, X_WIDTH), lambda i, *_: (jnp.minimum(i, last), 0)),
        ],
        out_specs=pl.BlockSpec(memory_space=pl.ANY),
        scratch_shapes=[
            pltpu.VMEM((2, TILE_SLOT_ROWS, X_WIDTH), bf16),
            pltpu.VMEM((GROUP_TILE, X_WIDTH), bf16),
            pltpu.SemaphoreType.DMA((2,)),
            pltpu.SemaphoreType.DMA,
        ],
    )
    return pl.pallas_call(
        functools.partial(_dispatch_kernel, n_tiles=nt, n_group_tiles=n_rows // GROUP_TILE),
        grid_spec=grid_spec,
        out_shape=jax.ShapeDtypeStruct((n_rows, X_WIDTH), bf16),
        compiler_params=pltpu.CompilerParams(
            dimension_semantics=("arbitrary",), vmem_limit_bytes=VMEM_LIMIT),
        name="dispatch",
    )(plan["piece_dest"], plan["n_pieces"], plan["pad_dest"], plan["n_pad"], plan["n_used"], dl, hna)


def _moe_kernel(rstart_ref, rtiles_ref, nused_ref, xs_ref, wgu_hbm, bgu_ref, wd_hbm, bd_ref, ys_ref,
                wgu_f32, wd_f32, wgu_bf, wd_bf, xbuf, ybuf, wsem, xsem, ysem, *, n_group_tiles):
    e = pl.program_id(0)
    n_t = rtiles_ref[e]
    row0 = rstart_ref[e]
    ws = e % 2

    def w_copies(ex, slot):
        return (pltpu.make_async_copy(wgu_hbm.at[ex], wgu_f32.at[slot], wsem.at[slot]),
                pltpu.make_async_copy(wd_hbm.at[ex], wd_f32.at[slot], wsem.at[slot]))

    def x_copy(t, s):
        src = pl.multiple_of(row0 + t * GROUP_TILE, GROUP_TILE)
        return pltpu.make_async_copy(xs_ref.at[pl.ds(src, GROUP_TILE)], xbuf.at[s], xsem.at[s])

    def y_copy(t, s):
        dst = pl.multiple_of(row0 + t * GROUP_TILE, GROUP_TILE)
        return pltpu.make_async_copy(ybuf.at[s], ys_ref.at[pl.ds(dst, GROUP_TILE)], ysem.at[s])

    def tail_copy(t):
        dst = pl.multiple_of(t * GROUP_TILE, GROUP_TILE)
        return pltpu.make_async_copy(ybuf.at[0], ys_ref.at[pl.ds(dst, GROUP_TILE)], ysem.at[0])

    @pl.when(e == 0)
    def _():
        for c in w_copies(0, 0):
            c.start()

    @pl.when(n_t > 0)
    def _():
        x_copy(0, 0).start()

    @pl.when(n_t > 1)
    def _():
        x_copy(1, 1).start()

    @pl.when(e + 1 < N_EXPERTS)
    def _():
        for c in w_copies(jnp.minimum(e + 1, N_EXPERTS - 1), 1 - ws):
            c.start()

    for c in w_copies(e, ws):
        c.wait()

    @pl.when(n_t > 0)
    def _():
        wgu_bf[...] = wgu_f32[ws].astype(bf16)
        wd_bf[...] = wd_f32[ws].astype(bf16)

        def body(t, carry):
            s = t % 2
            x_copy(t, s).wait()

            @pl.when(t >= 2)
            def _():
                y_copy(t - 2, s).wait()

            h = jnp.dot(xbuf[s, :, :D_MODEL], wgu_bf[...], preferred_element_type=f32) + bgu_ref[0]
            hg = jnp.minimum(h[:, :D_FF], SWIGLU_LIMIT)
            hu = jnp.clip(h[:, D_FF:], -SWIGLU_LIMIT, SWIGLU_LIMIT)
            act = (hu + 1.0) * (hg * (1.0 / (1.0 + jnp.exp(-SWIGLU_ALPHA * hg))))
            y = jnp.dot(act.astype(bf16), wd_bf[...], preferred_element_type=f32) + bd_ref[0]
            aug = xbuf[s, :, D_MODEL:].astype(f32)
            lane = lax.broadcasted_iota(jnp.int32, aug.shape, 1)
            gate = jnp.sum(jnp.where((lane & (N_EXPERTS - 1)) == e, aug, 0.0), axis=1, keepdims=True)
            ybuf[s] = (gate * y).astype(ybuf.dtype)
            y_copy(t, s).start()

            @pl.when(t + 2 < n_t)
            def _():
                x_copy(t + 2, s).start()

            return carry

        lax.fori_loop(0, n_t, body, 0)

        @pl.when(n_t >= 2)
        def _():
            y_copy(n_t - 2, n_t % 2).wait()

        y_copy(n_t - 1, (n_t - 1) % 2).wait()

    @pl.when(e == N_EXPERTS - 1)
    def _():
        ybuf[0] = jnp.zeros(ybuf.shape[1:], ybuf.dtype)

        def start_tail(t, carry):
            tail_copy(t).start()
            return carry

        def wait_tail(t, carry):
            tail_copy(t).wait()
            return carry

        lax.fori_loop(nused_ref[0], n_group_tiles, start_tail, 0)
        lax.fori_loop(nused_ref[0], n_group_tiles, wait_tail, 0)


def _moe(xs, plan, w_gu, b_gu, w_down, b_down):
    n_rows = xs.shape[0]
    tg = GROUP_TILE
    grid_spec = pltpu.PrefetchScalarGridSpec(
        num_scalar_prefetch=3,
        grid=(N_EXPERTS,),
        in_specs=[
            pl.BlockSpec(memory_space=pl.ANY),
            pl.BlockSpec(memory_space=pl.ANY),
            pl.BlockSpec((1, 1, 2 * D_FF), lambda e, *_: (e, 0, 0)),
            pl.BlockSpec(memory_space=pl.ANY),
            pl.BlockSpec((1, 1, D_MODEL), lambda e, *_: (e, 0, 0)),
        ],
        out_specs=pl.BlockSpec(memory_space=pl.ANY),
        scratch_shapes=[
            pltpu.VMEM((2, D_MODEL, 2 * D_FF), f32),
            pltpu.VMEM((2, D_FF, D_MODEL), f32),
            pltpu.VMEM((D_MODEL, 2 * D_FF), bf16),
            pltpu.VMEM((D_FF, D_MODEL), bf16),
            pltpu.VMEM((2, tg, X_WIDTH), bf16),
            pltpu.VMEM((2, tg, D_MODEL), bf16),
            pltpu.SemaphoreType.DMA((2,)),
            pltpu.SemaphoreType.DMA((2,)),
            pltpu.SemaphoreType.DMA((2,)),
        ],
    )
    return pl.pallas_call(
        functools.partial(_moe_kernel, n_group_tiles=n_rows // tg),
        grid_spec=grid_spec,
        out_shape=jax.ShapeDtypeStruct((n_rows, D_MODEL), bf16),
        compiler_params=pltpu.CompilerParams(
            dimension_semantics=("arbitrary",), vmem_limit_bytes=VMEM_LIMIT),
        name="moe_experts",
    )(plan["region_start"], plan["region_tiles"], plan["n_used"], xs, w_gu,
      b_gu.reshape(N_EXPERTS, 1, 2 * D_FF), w_down, b_down.reshape(N_EXPERTS, 1, D_MODEL))


def _combine_kernel(pdest_ref, npiece_ref, h1_ref, col_ref, g_ref, ys_ref, o_ref, ybuf, sems,
                    *, n_tiles):
    i = pl.program_id(0)
    slot = i % 2

    def piece_copy(t, s, j):
        src = pl.multiple_of(pdest_ref[t, j], SLOT_ROWS)
        return pltpu.make_async_copy(
            ys_ref.at[pl.ds(src, SLOT_ROWS)],
            ybuf.at[s, pl.ds(pl.multiple_of(j * SLOT_ROWS, SLOT_ROWS), SLOT_ROWS)], sems.at[s])

    def fetch(t, s):
        def start(j, c):
            piece_copy(t, s, j).start()
            return c
        lax.fori_loop(0, npiece_ref[t], start, 0)

    @pl.when(i == 0)
    def _():
        ybuf[...] = jnp.zeros_like(ybuf)
        fetch(0, 0)

    @pl.when(i + 1 < n_tiles)
    def _():
        fetch(jnp.minimum(i + 1, n_tiles - 1), 1 - slot)

    def wait(j, c):
        piece_copy(i, slot, j).wait()
        return c
    lax.fori_loop(0, npiece_ref[i], wait, 0)

    col = col_ref[...]
    rows = [col[:, 3 * N_EXPERTS + k:3 * N_EXPERTS + k + 1].astype(jnp.int32) for k in range(TOP_K)]
    h = h1_ref[...]
    for c in range(TILE_SLOT_ROWS // PERM_CHUNK):
        r = lax.broadcasted_iota(jnp.int32, (col.shape[0], PERM_CHUNK), 1) + c * PERM_CHUNK
        g = jnp.zeros(r.shape, f32)
        for k in range(TOP_K):
            g = jnp.where(rows[k] == r, 1.0, g)
        h = h + jnp.dot(g.astype(bf16), ybuf[slot, c * PERM_CHUNK:(c + 1) * PERM_CHUNK, :],
                        preferred_element_type=f32)
    o_ref[...] = h * lax.rsqrt(jnp.mean(h * h, axis=-1, keepdims=True) + RMS_EPS) * g_ref[...]


def _combine(h1, ys, col, plan, norm_final_g):
    n = h1.shape[0]
    tm = ROW_TILE
    nt = n // tm
    grid_spec = pltpu.PrefetchScalarGridSpec(
        num_scalar_prefetch=2,
        grid=(nt,),
        in_specs=[
            pl.BlockSpec((tm, D_MODEL), lambda i, *_: (i, 0)),
            pl.BlockSpec((tm, LANES), lambda i, *_: (i, 0)),
            pl.BlockSpec((1, D_MODEL), lambda i, *_: (0, 0)),
            pl.BlockSpec(memory_space=pl.ANY),
        ],
        out_specs=pl.BlockSpec((tm, D_MODEL), lambda i, *_: (i, 0)),
        scratch_shapes=[
            pltpu.VMEM((2, TILE_SLOT_ROWS, D_MODEL), bf16),
            pltpu.SemaphoreType.DMA((2,)),
        ],
    )
    return pl.pallas_call(
        functools.partial(_combine_kernel, n_tiles=nt),
        grid_spec=grid_spec,
        out_shape=jax.ShapeDtypeStruct((n, D_MODEL), f32),
        compiler_params=pltpu.CompilerParams(
            dimension_semantics=("arbitrary",), vmem_limit_bytes=VMEM_LIMIT),
        name="combine",
    )(plan["piece_dest"], plan["n_pieces"], h1, col, norm_final_g.reshape(1, D_MODEL), ys)


def _routing_plan(slot_counts):
    tg = GROUP_TILE
    i32 = jnp.int32
    pc = slot_counts.astype(i32) * SLOT_ROWS
    local_end = jnp.cumsum(pc, axis=1)
    local_start = local_end - pc
    total = jnp.sum(pc, axis=0)
    region = ((total + tg - 1) // tg) * tg
    region_end = jnp.cumsum(region)
    region_start = region_end - region
    base = region_start[None, :] + jnp.cumsum(pc, axis=0) - pc
    n_pieces = local_end[:, -1] // SLOT_ROWS

    piece_row = jnp.arange(PIECES_PER_TILE, dtype=i32) * SLOT_ROWS
    owner = jnp.sum((local_end[:, None, :] <= piece_row[None, :, None]).astype(i32), axis=2)
    owner_hit = owner[:, :, None] == jnp.arange(N_EXPERTS, dtype=i32)[None, None, :]
    shift = jnp.sum(jnp.where(owner_hit, (base - local_start)[:, None, :], 0), axis=2)
    piece_dest = shift + piece_row[None, :]
    piece_dest = jnp.where(piece_row[None, :] < local_end[:, -1:], piece_dest, 0)

    pad_slots = (region - total) // SLOT_ROWS
    pad_end = jnp.cumsum(pad_slots)
    m = jnp.arange(N_EXPERTS * (tg // SLOT_ROWS), dtype=i32)
    pad_owner = jnp.minimum(jnp.sum((pad_end[None, :] <= m[:, None]).astype(i32), axis=1),
                            N_EXPERTS - 1)
    pad_hit = pad_owner[:, None] == jnp.arange(N_EXPERTS, dtype=i32)[None, :]
    pad_first = jnp.sum(jnp.where(pad_hit, (region_start + total - (pad_end - pad_slots) * SLOT_ROWS)
                                  [None, :], 0), axis=1)
    pad_dest = jnp.where(m < pad_end[-1], pad_first + m * SLOT_ROWS, 0)

    return {
        "piece_dest": piece_dest, "n_pieces": n_pieces, "pad_dest": pad_dest, "n_pad": pad_end[-1:],
        "region_start": region_start, "region_tiles": region // tg,
        "n_used": (region_end[-1] // tg).reshape(1),
    }


def kernel(x, norm_mix_g, w_in, b_f, b_glu, w_dw, b_dw, ln_g, ln_b, w_out, norm_ffn_g, w_router,
           b_router, w_gu, b_gu, w_down, b_down, norm_final_g):
    batch, seq, d = x.shape
    n = batch * seq
    x2d = x.reshape(n, d)

    qa, kta, v, u = _inproj(x2d, norm_mix_g, w_in, b_f, b_glu, batch=batch, seq=seq)
    att = _attention(qa, kta, v.reshape(batch, seq, D_ATT), batch=batch, seq=seq)
    conv = _conv_module(u.reshape(batch, seq, D_CONV), w_dw, b_dw, ln_g, ln_b, batch=batch, seq=seq)

    h1, hna, dl, col, slot_counts = _outproj_router(
        att.reshape(n, D_ATT), conv.reshape(n, D_CONV), x2d, w_out, norm_ffn_g, w_router, b_router)

    n_token_tiles = n // ROW_TILE
    max_rows = n * TOP_K + N_EXPERTS * (n_token_tiles * (SLOT_ROWS - 1) + GROUP_TILE - 1)
    n_group_tiles = -(-max_rows // GROUP_TILE)
    plan = _routing_plan(slot_counts[:, :, 0])
    xs = _dispatch(hna, dl, plan, n_group_tiles * GROUP_TILE)
    ys = _moe(xs, plan, w_gu, b_gu, w_down, b_down)
    out = _combine(h1, ys, col, plan, norm_final_g)
    return out.reshape(batch, seq, d)
```

```python
import functools
import math

import numpy as np
import jax
import jax.numpy as jnp
from jax import lax
from jax.experimental import pallas as pl
from jax.experimental.pallas import tpu as pltpu

D_MODEL = 1024
ATT_HEADS = 8
HEAD_DIM = 64
D_ATT = ATT_HEADS * HEAD_DIM
D_CONV = D_MODEL - D_ATT
CONV_WIDTH = 31
N_EXPERTS = 32
TOP_K = 4
D_FF = 1024
SWIGLU_LIMIT = 7.0
SWIGLU_ALPHA = 1.702
RMS_EPS = 1e-6
LN_EPS = 1e-5
MASK_VALUE = -1e30

LANES = 128
SUBLANES = 8
CONV_ROWS = 64
ROW_TILE = 512
ATT_SUB = 512
ATT_Q_TILES = 2
CONV_HALO = 32
GROUP_TILE = 256
SLOT_ROWS = 16
X_WIDTH = D_MODEL + LANES
TILE_SLOT_ROWS = ROW_TILE * TOP_K + N_EXPERTS * SLOT_ROWS
PIECES_PER_TILE = TILE_SLOT_ROWS // SLOT_ROWS
MIN_PIECES = ROW_TILE * TOP_K // SLOT_ROWS
PERM_CHUNK = 512
VMEM_LIMIT = 56 * 1024 * 1024

AUG_Q = HEAD_DIM
AUG_K = HEAD_DIM + 3
ONES_LANE = ATT_HEADS

f32 = jnp.float32
bf16 = jnp.bfloat16


def _split3(x):
    hi = x.astype(bf16)
    r1 = x - hi.astype(f32)
    mid = r1.astype(bf16)
    lo = (r1 - mid.astype(f32)).astype(bf16)
    return hi, mid, lo


def _selection_matrices():
    sel_q = np.zeros((3, LANES, ATT_HEADS * LANES), np.float32)
    sel_k = np.zeros((3, ATT_HEADS * LANES, LANES), np.float32)
    for h in range(ATT_HEADS):
        base = h * LANES
        for p in range(3):
            sel_q[p, h, base + AUG_Q + p] = 1.0
            sel_q[0, ONES_LANE, base + AUG_K + p] = 1.0
            sel_k[p, base + AUG_K + p, h] = -1.0
            sel_k[0, base + AUG_Q + p, ONES_LANE] = 1.0
    return jnp.asarray(sel_q, bf16), jnp.asarray(sel_k, bf16)


def _inproj_kernel(x_ref, g_ref, wq_ref, wkt_ref, wv_ref, wf_ref, wglu_ref, bf_ref, bglu_ref,
                   selq_ref, selk_ref, tri_ref, wdw_ref, bdw_ref, lng_ref, lnb_ref,
                   qa_ref, kta_ref, v_ref, conv_ref, carry_ref, ext_ref, *, tiles_per_seq):
    i = pl.program_id(0)
    tm = x_ref.shape[0]

    @pl.when(i % tiles_per_seq == 0)
    def _():
        carry_ref[...] = jnp.zeros_like(carry_ref)
        ext_ref[0, 0:CONV_HALO, :] = jnp.zeros((CONV_HALO, D_CONV), f32)

    x = x_ref[...]
    xn = x * lax.rsqrt(jnp.mean(x * x, axis=-1, keepdims=True) + RMS_EPS) * g_ref[...]
    xb = xn.astype(bf16)

    glu = jnp.dot(xb, wglu_ref[...], preferred_element_type=f32) + bglu_ref[...]
    ext_ref[0, CONV_HALO:, :] = glu[:, :D_CONV] * (1.0 / (1.0 + jnp.exp(-glu[:, D_CONV:])))
    span = tm + CONV_HALO - SUBLANES
    for s in range(1, SUBLANES):
        ext_ref[s, 0:span, :] = ext_ref[0, s:s + span, :]
    shift = CONV_HALO - (CONV_WIDTH - 1)
    for cb in range(tm // CONV_ROWS):
        r0 = cb * CONV_ROWS
        acc = jnp.zeros((CONV_ROWS, D_CONV), f32) + bdw_ref[...]
        for j in range(CONV_WIDTH):
            s = (shift + j) % SUBLANES
            a = shift + j - s
            acc = acc + wdw_ref[j:j + 1, :] * ext_ref[s, r0 + a:r0 + a + CONV_ROWS, :]
        mu = jnp.mean(acc, axis=-1, keepdims=True)
        d = acc - mu
        var = jnp.mean(d * d, axis=-1, keepdims=True)
        y = d * lax.rsqrt(var + LN_EPS) * lng_ref[...] + lnb_ref[...]
        conv_ref[r0:r0 + CONV_ROWS, :] = (y * (1.0 / (1.0 + jnp.exp(-y)))).astype(conv_ref.dtype)
    ext_ref[0, 0:CONV_HALO, :] = ext_ref[0, tm:tm + CONV_HALO, :]

    f = jnp.dot(xb, wf_ref[...], preferred_element_type=f32) + bf_ref[...]
    log_f = jnp.minimum(f, 0.0) - jnp.log1p(jnp.exp(-jnp.abs(f)))
    lane = lax.broadcasted_iota(jnp.int32, log_f.shape, 1)
    log_f = jnp.where(lane < ATT_HEADS, log_f, 0.0)
    tri = tri_ref[...]
    hi, mid, lo = _split3(log_f)
    c = (jnp.dot(tri, hi, preferred_element_type=f32)
         + jnp.dot(tri, mid, preferred_element_type=f32)
         + jnp.dot(tri, lo, preferred_element_type=f32)) + carry_ref[...]
    carry_ref[...] = c[tm - 1:tm, :]

    c_hi, c_mid, c_lo = _split3(c)
    c_hi = jnp.where(lane == ONES_LANE, jnp.ones_like(c_hi), c_hi)
    qa = jnp.dot(xb, wq_ref[...], preferred_element_type=f32) * (1.0 / math.sqrt(HEAD_DIM))
    qa = (qa + jnp.dot(c_hi, selq_ref[0], preferred_element_type=f32)
          + jnp.dot(c_mid, selq_ref[1], preferred_element_type=f32)
          + jnp.dot(c_lo, selq_ref[2], preferred_element_type=f32)).astype(bf16)
    for h in range(ATT_HEADS):
        qa_ref[0, h] = qa[:, h * LANES:(h + 1) * LANES]

    ct = c.T
    ct_hi, ct_mid, ct_lo = _split3(ct)
    row = lax.broadcasted_iota(jnp.int32, ct.shape, 0)
    ct_hi = jnp.where(row == ONES_LANE, jnp.ones_like(ct_hi), ct_hi)
    kta = lax.dot_general(wkt_ref[...], xb, (((1,), (1,)), ((), ())), preferred_element_type=f32)
    kta = (kta + jnp.dot(selk_ref[0], ct_hi, preferred_element_type=f32)
           + jnp.dot(selk_ref[1], ct_mid, preferred_element_type=f32)
           + jnp.dot(selk_ref[2], ct_lo, preferred_element_type=f32)).astype(bf16)
    for h in range(ATT_HEADS):
        kta_ref[0, h, 0] = kta[h * LANES:(h + 1) * LANES, :]

    v_ref[...] = jnp.dot(xb, wv_ref[...], preferred_element_type=f32).astype(bf16)


def _inproj(x2d, norm_mix_g, w_in, b_f, b_glu, w_dw, b_dw, ln_g, ln_b, *, batch, seq):
    n = batch * seq
    tm = ROW_TILE
    nt = seq // tm
    wq = w_in[:, :D_ATT]
    wk = w_in[:, D_ATT:2 * D_ATT]
    wv = w_in[:, 2 * D_ATT:3 * D_ATT]
    wf = w_in[:, 3 * D_ATT:3 * D_ATT + ATT_HEADS]
    wglu = w_in[:, 3 * D_ATT + ATT_HEADS:]
    pad = LANES - HEAD_DIM
    wq_pad = jnp.pad(wq.reshape(D_MODEL, ATT_HEADS, HEAD_DIM), ((0, 0), (0, 0), (0, pad)))
    wq_pad = wq_pad.reshape(D_MODEL, ATT_HEADS * LANES).astype(bf16)
    wkt_pad = jnp.pad(wk.T.reshape(ATT_HEADS, HEAD_DIM, D_MODEL), ((0, 0), (0, pad), (0, 0)))
    wkt_pad = wkt_pad.reshape(ATT_HEADS * LANES, D_MODEL).astype(bf16)
    wf_pad = jnp.pad(wf, ((0, 0), (0, LANES - ATT_HEADS))).astype(bf16)
    bf_pad = jnp.pad(b_f, (0, LANES - ATT_HEADS)).reshape(1, LANES)
    sel_q, sel_k = _selection_matrices()
    tri = jnp.asarray(np.tril(np.ones((tm, tm), np.float32)), bf16)

    const = lambda shape: pl.BlockSpec(shape, lambda i: (0,) * len(shape))
    return pl.pallas_call(
        functools.partial(_inproj_kernel, tiles_per_seq=nt),
        grid=(n // tm,),
        in_specs=[
            pl.BlockSpec((tm, D_MODEL), lambda i: (i, 0)),
            const((1, D_MODEL)),
            const((D_MODEL, ATT_HEADS * LANES)),
            const((ATT_HEADS * LANES, D_MODEL)),
            const((D_MODEL, D_ATT)),
            const((D_MODEL, LANES)),
            const((D_MODEL, 2 * D_CONV)),
            const((1, LANES)),
            const((1, 2 * D_CONV)),
            const((3, LANES, ATT_HEADS * LANES)),
            const((3, ATT_HEADS * LANES, LANES)),
            const((tm, tm)),
            const((CONV_HALO, D_CONV)), const((1, D_CONV)), const((1, D_CONV)), const((1, D_CONV)),
        ],
        out_specs=[
            pl.BlockSpec((1, ATT_HEADS, tm, LANES), lambda i: (i // nt, 0, i % nt, 0)),
            pl.BlockSpec((1, ATT_HEADS, 1, LANES, tm), lambda i: (i // nt, 0, i % nt, 0, 0)),
            pl.BlockSpec((tm, D_ATT), lambda i: (i, 0)),
            pl.BlockSpec((tm, D_CONV), lambda i: (i, 0)),
        ],
        out_shape=[
            jax.ShapeDtypeStruct((batch, ATT_HEADS, seq, LANES), bf16),
            jax.ShapeDtypeStruct((batch, ATT_HEADS, nt, LANES, tm), bf16),
            jax.ShapeDtypeStruct((n, D_ATT), bf16),
            jax.ShapeDtypeStruct((n, D_CONV), bf16),
        ],
        scratch_shapes=[pltpu.VMEM((1, LANES), f32),
                        pltpu.VMEM((SUBLANES, tm + CONV_HALO, D_CONV), f32)],
        compiler_params=pltpu.CompilerParams(
            dimension_semantics=("arbitrary",), vmem_limit_bytes=VMEM_LIMIT),
        name="inproj",
    )(x2d, norm_mix_g.reshape(1, D_MODEL), wq_pad, wkt_pad, wv.astype(bf16), wf_pad,
      wglu.astype(bf16), bf_pad, b_glu.reshape(1, 2 * D_CONV), sel_q, sel_k, tri,
      jnp.pad(w_dw, ((0, CONV_HALO - CONV_WIDTH), (0, 0))), b_dw.reshape(1, D_CONV),
      ln_g.reshape(1, D_CONV), ln_b.reshape(1, D_CONV))


def _attn_kernel(qa_ref, kta_ref, v_ref, o_ref, acc_ref, m_ref, *, tile):
    i = pl.program_id(2)
    q_rows = ATT_Q_TILES * tile
    first_half = lax.broadcasted_iota(jnp.int32, (tile, LANES), 1) < HEAD_DIM
    acc_ref[...] = jnp.zeros_like(acc_ref)
    m_ref[...] = jnp.full_like(m_ref, MASK_VALUE)
    sub = ATT_SUB
    blocks_per_tile = tile // sub

    def process(kv_tiles):
        v_augs = []
        for j, _ in kv_tiles:
            vp = v_ref[0, pl.ds(pl.multiple_of(j * tile, tile), tile), :]
            one = jnp.ones_like(vp)
            v_augs.append((jnp.where(first_half, vp, one), jnp.where(first_half, one, vp)))
        chains = [(t, h, rb) for t in range(len(kv_tiles)) for h in range(2)
                  for rb in range(q_rows // sub) if kv_tiles[t][1][rb // blocks_per_tile] is not None]

        def visible_width(t, rb):
            if kv_tiles[t][1][rb // blocks_per_tile] == "diag":
                return (rb % blocks_per_tile + 1) * sub
            return tile

        def scores(t, h, rb):
            return jnp.dot(qa_ref[0, h, rb * sub:(rb + 1) * sub, :],
                           kta_ref[0, h, kv_tiles[t][0]][:, :visible_width(t, rb)],
                           preferred_element_type=f32)

        s_next = scores(*chains[0])
        for ci, (t, h, rb) in enumerate(chains):
            s = s_next
            if ci + 1 < len(chains):
                s_next = scores(*chains[ci + 1])
            rows = slice(rb * sub, (rb + 1) * sub)
            width = s.shape[1]
            if kv_tiles[t][1][rb // blocks_per_tile] == "diag":
                r = lax.broadcasted_iota(jnp.int32, s.shape, 0) + (rb % blocks_per_tile) * sub
                c = lax.broadcasted_iota(jnp.int32, s.shape, 1)
                s = jnp.where(c <= r, s, MASK_VALUE)
            m_old = m_ref[h, rows, :]
            m_new = jnp.maximum(m_old, jnp.max(s, axis=-1, keepdims=True))
            alpha = jnp.exp(m_old - m_new)
            p = jnp.exp(s - jnp.concatenate([m_new] * (width // LANES), axis=1)).astype(bf16)
            acc_ref[h, rows, :] = alpha * acc_ref[h, rows, :] + jnp.dot(
                p, v_augs[t][h][:width, :], preferred_element_type=f32)
            m_ref[h, rows, :] = m_new

    everything = ("full",) * ATT_Q_TILES

    def body(jj, carry):
        process([(ATT_Q_TILES * jj + t, everything) for t in range(ATT_Q_TILES)])
        return carry

    lax.fori_loop(0, i, body, 0)
    process([(ATT_Q_TILES * i + t,
              tuple("full" if t < qt else ("diag" if t == qt else None) for qt in range(ATT_Q_TILES)))
             for t in range(ATT_Q_TILES)])

    for qt in range(ATT_Q_TILES):
        a0 = acc_ref[0, qt * tile:(qt + 1) * tile, :]
        a1 = acc_ref[1, qt * tile:(qt + 1) * tile, :]
        l0 = a0[:, HEAD_DIM:HEAD_DIM + 1]
        l1 = a1[:, 0:1]
        o_ref[0, qt * tile:(qt + 1) * tile, :] = jnp.where(
            first_half, a0 / l0, a1 / l1).astype(o_ref.dtype)


def _attention(qa, kta, v3, *, batch, seq):
    tile = ROW_TILE
    nt = seq // tile
    q_rows = ATT_Q_TILES * tile
    return pl.pallas_call(
        functools.partial(_attn_kernel, tile=tile),
        grid=(batch, ATT_HEADS // 2, seq // q_rows),
        in_specs=[
            pl.BlockSpec((1, 2, q_rows, LANES), lambda b, hp, i: (b, hp, i, 0)),
            pl.BlockSpec((1, 2, nt, LANES, tile), lambda b, hp, i: (b, hp, 0, 0, 0)),
            pl.BlockSpec((1, seq, LANES), lambda b, hp, i: (b, 0, hp)),
        ],
        out_specs=pl.BlockSpec((1, q_rows, LANES), lambda b, hp, i: (b, i, hp)),
        out_shape=jax.ShapeDtypeStruct((batch, seq, D_ATT), bf16),
        scratch_shapes=[pltpu.VMEM((2, q_rows, LANES), f32), pltpu.VMEM((2, q_rows, LANES), f32)],
        compiler_params=pltpu.CompilerParams(
            dimension_semantics=("arbitrary", "arbitrary", "arbitrary"),
            vmem_limit_bytes=VMEM_LIMIT),
        name="attention",
    )(qa, kta, v3)


def _outproj_router_kernel(att_ref, conv_ref, x_ref, wo_ref, g_ref, wr_hi_ref, wr_lo_ref, br_ref,
                           triu_ref, tril_ref, h1_ref, hna_ref, dl_ref, col_ref, cnt_ref):
    h1 = (x_ref[...]
          + jnp.dot(att_ref[...], wo_ref[:D_ATT, :], preferred_element_type=f32)
          + jnp.dot(conv_ref[...], wo_ref[D_ATT:, :], preferred_element_type=f32))
    h1_ref[...] = h1
    hn = h1 * lax.rsqrt(jnp.mean(h1 * h1, axis=-1, keepdims=True) + RMS_EPS) * g_ref[...]

    hn_hi = hn.astype(bf16)
    hn_lo = (hn - hn_hi.astype(f32)).astype(bf16)
    nt = (((1,), (1,)), ((), ()))
    logits = (lax.dot_general(wr_hi_ref[...], hn_hi, nt, preferred_element_type=f32)
              + lax.dot_general(wr_hi_ref[...], hn_lo, nt, preferred_element_type=f32)
              + lax.dot_general(wr_lo_ref[...], hn_hi, nt, preferred_element_type=f32)
              + br_ref[...])

    erow = lax.broadcasted_iota(jnp.int32, logits.shape, 0)
    work = logits
    vals, idxs = [], []
    for _ in range(TOP_K):
        mk = jnp.max(work, axis=0, keepdims=True)
        ik = jnp.min(jnp.where(work == mk, erow, N_EXPERTS), axis=0, keepdims=True)
        work = jnp.where(erow == ik, -jnp.inf, work)
        vals.append(mk)
        idxs.append(ik)
    exps = [jnp.exp(v - vals[0]) for v in vals]
    denom = exps[0] + exps[1] + exps[2] + exps[3]

    onehot = jnp.zeros(logits.shape, f32)
    gates = jnp.zeros(logits.shape, f32)
    for k in range(TOP_K):
        hit = erow == idxs[k]
        onehot = onehot + jnp.where(hit, 1.0, 0.0)
        gates = gates + jnp.where(hit, exps[k] / denom, 0.0)

    before = jnp.dot(onehot.astype(bf16), triu_ref[...], preferred_element_type=f32)
    count = jnp.sum(onehot, axis=1, keepdims=True)
    slots = jnp.floor((count + (SLOT_ROWS - 1)) * (1.0 / SLOT_ROWS))
    slots_b = jnp.broadcast_to(slots, (N_EXPERTS, LANES))
    slot_start = jnp.dot(tril_ref[...], slots_b.astype(bf16), preferred_element_type=f32)
    cnt_ref[0] = slots_b
    pos = before + slot_start[:, 0:1] * SLOT_ROWS
    dls = []
    for k in range(TOP_K):
        dk = jnp.sum(jnp.where(erow == idxs[k], pos, 0.0), axis=0, keepdims=True)
        dl_ref[k:k + 1, :] = dk.astype(jnp.int32)
        dls.append(dk)

    g_hi, g_mid, g_lo = _split3(gates)
    tm = logits.shape[1]
    r8 = lax.broadcasted_iota(jnp.int32, (8, tm), 0)
    dl8 = jnp.zeros((8, tm), f32)
    for k in range(TOP_K):
        dl8 = jnp.where(r8 == k, dls[k], dl8)
    stack = jnp.concatenate(
        [g_hi.astype(f32), g_mid.astype(f32), g_lo.astype(f32), dl8,
         jnp.zeros((LANES - 3 * N_EXPERTS - 8, tm), f32)], axis=0)
    col = stack.T
    col_ref[...] = col
    lane = lax.broadcasted_iota(jnp.int32, col.shape, 1)
    hna_ref[:, :D_MODEL] = hn_hi
    hna_ref[:, D_MODEL:] = jnp.where(lane < 3 * N_EXPERTS, col, 0.0).astype(bf16)


def _outproj_router(att2d, conv2d, x2d, w_out, norm_ffn_g, w_router, b_router):
    n = x2d.shape[0]
    tm = ROW_TILE
    wr_t = w_router.T
    wr_hi = wr_t.astype(bf16)
    wr_lo = (wr_t - wr_hi.astype(f32)).astype(bf16)
    triu = jnp.asarray(np.triu(np.ones((tm, tm), np.float32), k=1), bf16)
    tril = jnp.asarray(np.tril(np.ones((N_EXPERTS, N_EXPERTS), np.float32), k=-1), bf16)
    const = lambda shape: pl.BlockSpec(shape, lambda i: (0,) * len(shape))
    rows = lambda width: pl.BlockSpec((tm, width), lambda i: (i, 0))
    return pl.pallas_call(
        _outproj_router_kernel,
        grid=(n // tm,),
        in_specs=[
            rows(D_ATT), rows(D_CONV), rows(D_MODEL),
            const((D_MODEL, D_MODEL)), const((1, D_MODEL)),
            const((N_EXPERTS, D_MODEL)), const((N_EXPERTS, D_MODEL)), const((N_EXPERTS, 1)),
            const((tm, tm)), const((N_EXPERTS, N_EXPERTS)),
        ],
        out_specs=[
            rows(D_MODEL), rows(X_WIDTH),
            pl.BlockSpec((TOP_K, tm), lambda i: (0, i)),
            rows(LANES),
            pl.BlockSpec((1, N_EXPERTS, LANES), lambda i: (i, 0, 0)),
        ],
        out_shape=[
            jax.ShapeDtypeStruct((n, D_MODEL), f32),
            jax.ShapeDtypeStruct((n, X_WIDTH), bf16),
            jax.ShapeDtypeStruct((TOP_K, n), jnp.int32),
            jax.ShapeDtypeStruct((n, LANES), f32),
            jax.ShapeDtypeStruct((n // tm, N_EXPERTS, LANES), f32),
        ],
        compiler_params=pltpu.CompilerParams(
            dimension_semantics=("arbitrary",), vmem_limit_bytes=VMEM_LIMIT),
        name="outproj_router",
    )(att2d, conv2d, x2d, w_out.astype(bf16), norm_ffn_g.reshape(1, D_MODEL), wr_hi, wr_lo,
      b_router.reshape(N_EXPERTS, 1), triu, tril)


def _dispatch_kernel(pdest_ref, npiece_ref, padd_ref, npad_ref, nused_ref, dl_ref, hna_ref, xs_ref,
                     xbuf, zbuf, sems, zsem, *, n_tiles, n_group_tiles):
    i = pl.program_id(0)
    slot = i % 2

    def piece_copy(t, s, j):
        dst = pl.multiple_of(pdest_ref[t * PIECES_PER_TILE + j], SLOT_ROWS)
        return pltpu.make_async_copy(
            xbuf.at[s, pl.ds(pl.multiple_of(j * SLOT_ROWS, SLOT_ROWS), SLOT_ROWS)],
            xs_ref.at[pl.ds(dst, SLOT_ROWS)], sems.at[s])

    def for_each_piece(t, s, action):
        for j in range(MIN_PIECES):
            action(piece_copy(t, s, j))

        def rest(j, c):
            action(piece_copy(t, s, j))
            return c
        lax.fori_loop(MIN_PIECES, npiece_ref[t], rest, 0)

    def pad_copy(m):
        dst = pl.multiple_of(padd_ref[m], SLOT_ROWS)
        return pltpu.make_async_copy(
            zbuf.at[pl.ds(0, SLOT_ROWS)], xs_ref.at[pl.ds(dst, SLOT_ROWS)], zsem)

    def tail_copy(t):
        dst = pl.multiple_of(t * GROUP_TILE, GROUP_TILE)
        return pltpu.make_async_copy(zbuf, xs_ref.at[pl.ds(dst, GROUP_TILE)], zsem)

    @pl.when(i >= 2)
    def _():
        for_each_piece(jnp.maximum(i - 2, 0), slot, lambda c: c.wait())

    @pl.when(i < n_tiles)
    def _():
        dl = dl_ref[...]
        rhs = hna_ref[...]
        for c in range(TILE_SLOT_ROWS // PERM_CHUNK):
            r = lax.broadcasted_iota(jnp.int32, (PERM_CHUNK, dl.shape[1]), 0) + c * PERM_CHUNK
            p = jnp.zeros(r.shape, f32)
            for k in range(TOP_K):
                p = jnp.where(dl[k:k + 1, :] == r, 1.0, p)
            xbuf[slot, c * PERM_CHUNK:(c + 1) * PERM_CHUNK, :] = jnp.dot(
                p.astype(bf16), rhs, preferred_element_type=f32).astype(bf16)

        for_each_piece(jnp.minimum(i, n_tiles - 1), slot, lambda c: c.start())

    @pl.when(i == n_tiles)
    def _():
        zbuf[...] = jnp.zeros_like(zbuf)

        def start(m, c):
            pad_copy(m).start()
            return c
        lax.fori_loop(0, npad_ref[0], start, 0)

        def start_tail(t, c):
            tail_copy(t).start()
            return c
        lax.fori_loop(nused_ref[0], n_group_tiles, start_tail, 0)

    @pl.when(i == n_tiles + 1)
    def _():
        def wait(m, c):
            pad_copy(m).wait()
            return c
        lax.fori_loop(0, npad_ref[0], wait, 0)

        def wait_tail(t, c):
            tail_copy(t).wait()
            return c
        lax.fori_loop(nused_ref[0], n_group_tiles, wait_tail, 0)


def _dispatch(hna, dl, plan, n_rows):
    n = hna.shape[0]
    tm = ROW_TILE
    nt = n // tm
    last = nt - 1
    grid_spec = pltpu.PrefetchScalarGridSpec(
        num_scalar_prefetch=5,
        grid=(nt + 2,),
        in_specs=[
            pl.BlockSpec((TOP_K, tm), lambda i, *_: (0, jnp.minimum(i, last))),
            pl.BlockSpec((tm, X_WIDTH), lambda i, *_: (jnp.minimum(i, last), 0)),
        ],
        out_specs=pl.BlockSpec(memory_space=pl.ANY),
        scratch_shapes=[
            pltpu.VMEM((2, TILE_SLOT_ROWS, X_WIDTH), bf16),
            pltpu.VMEM((GROUP_TILE, X_WIDTH), bf16),
            pltpu.SemaphoreType.DMA((2,)),
            pltpu.SemaphoreType.DMA,
        ],
    )
    return pl.pallas_call(
        functools.partial(_dispatch_kernel, n_tiles=nt, n_group_tiles=n_rows // GROUP_TILE),
        grid_spec=grid_spec,
        out_shape=jax.ShapeDtypeStruct((n_rows, X_WIDTH), bf16),
        compiler_params=pltpu.CompilerParams(
            dimension_semantics=("arbitrary",), vmem_limit_bytes=VMEM_LIMIT),
        name="dispatch",
    )(plan["piece_dest"], plan["n_pieces"], plan["pad_dest"], plan["n_pad"], plan["n_used"], dl, hna)


def _moe_kernel(rstart_ref, rtiles_ref, nused_ref, xs_ref, wgu_hbm, bgu_ref, wd_hbm, bd_ref, ys_ref,
                wgu_f32, wd_f32, wgu_bf, wd_bf, xbuf, ybuf, wsem, xsem, ysem, *, n_group_tiles):
    e = pl.program_id(0)
    n_t = rtiles_ref[e]
    row0 = rstart_ref[e]
    ws = e % 2

    def w_copies(ex, slot):
        return (pltpu.make_async_copy(wgu_hbm.at[ex], wgu_f32.at[slot], wsem.at[slot]),
                pltpu.make_async_copy(wd_hbm.at[ex], wd_f32.at[slot], wsem.at[slot]))

    def x_copy(t, s):
        src = pl.multiple_of(row0 + t * GROUP_TILE, GROUP_TILE)
        return pltpu.make_async_copy(xs_ref.at[pl.ds(src, GROUP_TILE)], xbuf.at[s], xsem.at[s])

    def y_copy(t, s):
        dst = pl.multiple_of(row0 + t * GROUP_TILE, GROUP_TILE)
        return pltpu.make_async_copy(ybuf.at[s], ys_ref.at[pl.ds(dst, GROUP_TILE)], ysem.at[s])

    def tail_copy(t):
        dst = pl.multiple_of(t * GROUP_TILE, GROUP_TILE)
        return pltpu.make_async_copy(ybuf.at[0], ys_ref.at[pl.ds(dst, GROUP_TILE)], ysem.at[0])

    @pl.when(e == 0)
    def _():
        for c in w_copies(0, 0):
            c.start()

    @pl.when(n_t > 0)
    def _():
        x_copy(0, 0).start()

    @pl.when(n_t > 1)
    def _():
        x_copy(1, 1).start()

    @pl.when(e + 1 < N_EXPERTS)
    def _():
        for c in w_copies(jnp.minimum(e + 1, N_EXPERTS - 1), 1 - ws):
            c.start()

    for c in w_copies(e, ws):
        c.wait()

    @pl.when(n_t > 0)
    def _():
        wgu_bf[...] = wgu_f32[ws].astype(bf16)
        wd_bf[...] = wd_f32[ws].astype(bf16)

        def body(t, carry):
            s = t % 2
            x_copy(t, s).wait()

            @pl.when(t >= 2)
            def _():
                y_copy(t - 2, s).wait()

            h = jnp.dot(xbuf[s, :, :D_MODEL], wgu_bf[...], preferred_element_type=f32) + bgu_ref[0]
            hg = jnp.minimum(h[:, :D_FF], SWIGLU_LIMIT)
            hu = jnp.clip(h[:, D_FF:], -SWIGLU_LIMIT, SWIGLU_LIMIT)
            act = (hu + 1.0) * (hg * (1.0 / (1.0 + jnp.exp(-SWIGLU_ALPHA * hg))))
            y = jnp.dot(act.astype(bf16), wd_bf[...], preferred_element_type=f32) + bd_ref[0]
            aug = xbuf[s, :, D_MODEL:].astype(f32)
            lane = lax.broadcasted_iota(jnp.int32, aug.shape, 1)
            gate = jnp.sum(jnp.where((lane & (N_EXPERTS - 1)) == e, aug, 0.0), axis=1, keepdims=True)
            ybuf[s] = (gate * y).astype(ybuf.dtype)
            y_copy(t, s).start()

            @pl.when(t + 2 < n_t)
            def _():
                x_copy(t + 2, s).start()

            return carry

        lax.fori_loop(0, n_t, body, 0)

        @pl.when(n_t >= 2)
        def _():
            y_copy(n_t - 2, n_t % 2).wait()

        y_copy(n_t - 1, (n_t - 1) % 2).wait()

    @pl.when(e == N_EXPERTS - 1)
    def _():
        ybuf[0] = jnp.zeros(ybuf.shape[1:], ybuf.dtype)

        def start_tail(t, carry):
            tail_copy(t).start()
            return carry

        def wait_tail(t, carry):
            tail_copy(t).wait()
            return carry

        lax.fori_loop(nused_ref[0], n_group_tiles, start_tail, 0)
        lax.fori_loop(nused_ref[0], n_group_tiles, wait_tail, 0)


def _moe(xs, plan, w_gu, b_gu, w_down, b_down):
    n_rows = xs.shape[0]
    tg = GROUP_TILE
    grid_spec = pltpu.PrefetchScalarGridSpec(
        num_scalar_prefetch=3,
        grid=(N_EXPERTS,),
        in_specs=[
            pl.BlockSpec(memory_space=pl.ANY),
            pl.BlockSpec(memory_space=pl.ANY),
            pl.BlockSpec((1, 1, 2 * D_FF), lambda e, *_: (e, 0, 0)),
            pl.BlockSpec(memory_space=pl.ANY),
            pl.BlockSpec((1, 1, D_MODEL), lambda e, *_: (e, 0, 0)),
        ],
        out_specs=pl.BlockSpec(memory_space=pl.ANY),
        scratch_shapes=[
            pltpu.VMEM((2, D_MODEL, 2 * D_FF), f32),
            pltpu.VMEM((2, D_FF, D_MODEL), f32),
            pltpu.VMEM((D_MODEL, 2 * D_FF), bf16),
            pltpu.VMEM((D_FF, D_MODEL), bf16),
            pltpu.VMEM((2, tg, X_WIDTH), bf16),
            pltpu.VMEM((2, tg, D_MODEL), bf16),
            pltpu.SemaphoreType.DMA((2,)),
            pltpu.SemaphoreType.DMA((2,)),
            pltpu.SemaphoreType.DMA((2,)),
        ],
    )
    return pl.pallas_call(
        functools.partial(_moe_kernel, n_group_tiles=n_rows // tg),
        grid_spec=grid_spec,
        out_shape=jax.ShapeDtypeStruct((n_rows, D_MODEL), bf16),
        compiler_params=pltpu.CompilerParams(
            dimension_semantics=("arbitrary",), vmem_limit_bytes=VMEM_LIMIT),
        name="moe_experts",
    )(plan["region_start"], plan["region_tiles"], plan["n_used"], xs, w_gu,
      b_gu.reshape(N_EXPERTS, 1, 2 * D_FF), w_down, b_down.reshape(N_EXPERTS, 1, D_MODEL))


def _combine_kernel(pdest_ref, npiece_ref, h1_ref, col_ref, g_ref, ys_ref, o_ref, ybuf, sems,
                    *, n_tiles):
    i = pl.program_id(0)
    slot = i % 2

    def piece_copy(t, s, j):
        src = pl.multiple_of(pdest_ref[t * PIECES_PER_TILE + j], SLOT_ROWS)
        return pltpu.make_async_copy(
            ys_ref.at[pl.ds(src, SLOT_ROWS)],
            ybuf.at[s, pl.ds(pl.multiple_of(j * SLOT_ROWS, SLOT_ROWS), SLOT_ROWS)], sems.at[s])

    def for_each_piece(t, s, action):
        for j in range(MIN_PIECES):
            action(piece_copy(t, s, j))

        def rest(j, c):
            action(piece_copy(t, s, j))
            return c
        lax.fori_loop(MIN_PIECES, npiece_ref[t], rest, 0)

    def fetch(t, s):
        for_each_piece(t, s, lambda c: c.start())

    @pl.when(i == 0)
    def _():
        ybuf[...] = jnp.zeros_like(ybuf)
        fetch(0, 0)

    @pl.when(i + 1 < n_tiles)
    def _():
        fetch(jnp.minimum(i + 1, n_tiles - 1), 1 - slot)

    for_each_piece(i, slot, lambda c: c.wait())

    col = col_ref[...]
    rows = [col[:, 3 * N_EXPERTS + k:3 * N_EXPERTS + k + 1].astype(jnp.int32) for k in range(TOP_K)]
    h = h1_ref[...]
    for c in range(TILE_SLOT_ROWS // PERM_CHUNK):
        r = lax.broadcasted_iota(jnp.int32, (col.shape[0], PERM_CHUNK), 1) + c * PERM_CHUNK
        g = jnp.zeros(r.shape, f32)
        for k in range(TOP_K):
            g = jnp.where(rows[k] == r, 1.0, g)
        h = h + jnp.dot(g.astype(bf16), ybuf[slot, c * PERM_CHUNK:(c + 1) * PERM_CHUNK, :],
                        preferred_element_type=f32)
    o_ref[...] = h * lax.rsqrt(jnp.mean(h * h, axis=-1, keepdims=True) + RMS_EPS) * g_ref[...]


def _combine(h1, ys, col, plan, norm_final_g):
    n = h1.shape[0]
    tm = ROW_TILE
    nt = n // tm
    grid_spec = pltpu.PrefetchScalarGridSpec(
        num_scalar_prefetch=2,
        grid=(nt,),
        in_specs=[
            pl.BlockSpec((tm, D_MODEL), lambda i, *_: (i, 0)),
            pl.BlockSpec((tm, LANES), lambda i, *_: (i, 0)),
            pl.BlockSpec((1, D_MODEL), lambda i, *_: (0, 0)),
            pl.BlockSpec(memory_space=pl.ANY),
        ],
        out_specs=pl.BlockSpec((tm, D_MODEL), lambda i, *_: (i, 0)),
        scratch_shapes=[
            pltpu.VMEM((2, TILE_SLOT_ROWS, D_MODEL), bf16),
            pltpu.SemaphoreType.DMA((2,)),
        ],
    )
    return pl.pallas_call(
        functools.partial(_combine_kernel, n_tiles=nt),
        grid_spec=grid_spec,
        out_shape=jax.ShapeDtypeStruct((n, D_MODEL), f32),
        compiler_params=pltpu.CompilerParams(
            dimension_semantics=("arbitrary",), vmem_limit_bytes=VMEM_LIMIT),
        name="combine",
    )(plan["piece_dest"], plan["n_pieces"], h1, col, norm_final_g.reshape(1, D_MODEL), ys)


def _routing_plan(slot_counts):
    tg = GROUP_TILE
    i32 = jnp.int32
    pc = slot_counts.astype(i32) * SLOT_ROWS
    local_end = jnp.cumsum(pc, axis=1)
    local_start = local_end - pc
    total = jnp.sum(pc, axis=0)
    region = ((total + tg - 1) // tg) * tg
    region_end = jnp.cumsum(region)
    region_start = region_end - region
    base = region_start[None, :] + jnp.cumsum(pc, axis=0) - pc
    n_pieces = local_end[:, -1] // SLOT_ROWS

    piece_row = jnp.arange(PIECES_PER_TILE, dtype=i32) * SLOT_ROWS
    owner = jnp.sum((local_end[:, None, :] <= piece_row[None, :, None]).astype(i32), axis=2)
    owner_hit = owner[:, :, None] == jnp.arange(N_EXPERTS, dtype=i32)[None, None, :]
    shift = jnp.sum(jnp.where(owner_hit, (base - local_start)[:, None, :], 0), axis=2)
    piece_dest = shift + piece_row[None, :]
    piece_dest = jnp.where(piece_row[None, :] < local_end[:, -1:], piece_dest, 0)

    pad_slots = (region - total) // SLOT_ROWS
    pad_end = jnp.cumsum(pad_slots)
    m = jnp.arange(N_EXPERTS * (tg // SLOT_ROWS), dtype=i32)
    pad_owner = jnp.minimum(jnp.sum((pad_end[None, :] <= m[:, None]).astype(i32), axis=1),
                            N_EXPERTS - 1)
    pad_hit = pad_owner[:, None] == jnp.arange(N_EXPERTS, dtype=i32)[None, :]
    pad_first = jnp.sum(jnp.where(pad_hit, (region_start + total - (pad_end - pad_slots) * SLOT_ROWS)
                                  [None, :], 0), axis=1)
    pad_dest = jnp.where(m < pad_end[-1], pad_first + m * SLOT_ROWS, 0)

    return {
        "piece_dest": piece_dest.reshape(-1), "n_pieces": n_pieces, "pad_dest": pad_dest,
        "n_pad": pad_end[-1:],
        "region_start": region_start, "region_tiles": region // tg,
        "n_used": (region_end[-1] // tg).reshape(1),
    }


def kernel(x, norm_mix_g, w_in, b_f, b_glu, w_dw, b_dw, ln_g, ln_b, w_out, norm_ffn_g, w_router,
           b_router, w_gu, b_gu, w_down, b_down, norm_final_g):
    batch, seq, d = x.shape
    assert d == D_MODEL and seq % (ATT_Q_TILES * ROW_TILE) == 0, (batch, seq, d)
    n = batch * seq
    x2d = x.reshape(n, d)

    qa, kta, v, conv = _inproj(x2d, norm_mix_g, w_in, b_f, b_glu, w_dw, b_dw, ln_g, ln_b,
                               batch=batch, seq=seq)
    att = _attention(qa, kta, v.reshape(batch, seq, D_ATT), batch=batch, seq=seq)

    h1, hna, dl, col, slot_counts = _outproj_router(
        att.reshape(n, D_ATT), conv, x2d, w_out, norm_ffn_g, w_router, b_router)

    n_token_tiles = n // ROW_TILE
    max_rows = n * TOP_K + N_EXPERTS * (n_token_tiles * (SLOT_ROWS - 1) + GROUP_TILE - 1)
    n_group_tiles = -(-max_rows // GROUP_TILE)
    plan = _routing_plan(slot_counts[:, :, 0])
    xs = _dispatch(hna, dl, plan, n_group_tiles * GROUP_TILE)
    ys = _moe(xs, plan, w_gu, b_gu, w_down, b_down)
    out = _combine(h1, ys, col, plan, norm_final_g)
    return out.reshape(batch, seq, d)
```

```python
import functools
import math

import numpy as np
import jax
import jax.numpy as jnp
from jax import lax
from jax.experimental import pallas as pl
from jax.experimental.pallas import tpu as pltpu

D_MODEL = 1024
ATT_HEADS = 8
HEAD_DIM = 64
D_ATT = ATT_HEADS * HEAD_DIM
D_CONV = D_MODEL - D_ATT
CONV_WIDTH = 31
N_EXPERTS = 32
TOP_K = 4
D_FF = 1024
SWIGLU_LIMIT = 7.0
SWIGLU_ALPHA = 1.702
RMS_EPS = 1e-6
LN_EPS = 1e-5
MASK_VALUE = -1e30
LOG2_E = 1.4426950408889634

LANES = 128
SUBLANES = 8
CONV_ROWS = 64
ROW_TILE = 512
ATT_SUB = 512
ATT_Q_TILES = 2
CONV_HALO = 32
GROUP_TILE = 256
SLOT_ROWS = 16
X_WIDTH = D_MODEL + LANES
TILE_SLOT_ROWS = ROW_TILE * TOP_K + N_EXPERTS * SLOT_ROWS
PIECES_PER_TILE = TILE_SLOT_ROWS // SLOT_ROWS
MIN_PIECES = ROW_TILE * TOP_K // SLOT_ROWS
PERM_CHUNK = 512
VMEM_LIMIT = 56 * 1024 * 1024

AUG_Q = HEAD_DIM
AUG_K = HEAD_DIM + 3
ONES_LANE = ATT_HEADS
PART_STRIDE = 16

f32 = jnp.float32
bf16 = jnp.bfloat16


def _split3(x):
    hi = x.astype(bf16)
    r1 = x - hi.astype(f32)
    mid = r1.astype(bf16)
    lo = (r1 - mid.astype(f32)).astype(bf16)
    return hi, mid, lo


def _selection_matrices():
    sel_q = np.zeros((LANES, ATT_HEADS * LANES), np.float32)
    sel_k = np.zeros((ATT_HEADS * LANES, LANES), np.float32)
    for h in range(ATT_HEADS):
        base = h * LANES
        for p in range(3):
            sel_q[PART_STRIDE * p + h, base + AUG_Q + p] = 1.0
            sel_q[ONES_LANE, base + AUG_K + p] = 1.0
            sel_k[base + AUG_K + p, PART_STRIDE * p + h] = -1.0
            sel_k[base + AUG_Q + p, ONES_LANE] = 1.0
    return jnp.asarray(sel_q, bf16), jnp.asarray(sel_k, bf16)


def _inproj_kernel(x_ref, g_ref, wq_ref, wkt_ref, wv_ref, wf_ref, wglu_ref, bf_ref, bglu_ref,
                   selq_ref, selk_ref, tri_ref, wdw_ref, bdw_ref, lng_ref, lnb_ref,
                   qa_ref, kta_ref, v_ref, conv_ref, carry_ref, ext_ref, *, tiles_per_seq):
    i = pl.program_id(0)
    tm = x_ref.shape[0]

    @pl.when(i % tiles_per_seq == 0)
    def _():
        carry_ref[...] = jnp.zeros_like(carry_ref)
        ext_ref[0, 0:CONV_HALO, :] = jnp.zeros((CONV_HALO, D_CONV), f32)

    x = x_ref[...]
    xn = x * lax.rsqrt(jnp.mean(x * x, axis=-1, keepdims=True) + RMS_EPS) * g_ref[...]
    xb = xn.astype(bf16)

    glu = jnp.dot(xb, wglu_ref[...], preferred_element_type=f32) + bglu_ref[...]
    ext_ref[0, CONV_HALO:, :] = glu[:, :D_CONV] * (1.0 / (1.0 + jnp.exp(-glu[:, D_CONV:])))
    span = tm + CONV_HALO - SUBLANES
    for s in range(1, SUBLANES):
        ext_ref[s, 0:span, :] = ext_ref[0, s:s + span, :]
    shift = CONV_HALO - (CONV_WIDTH - 1)
    for cb in range(tm // CONV_ROWS):
        r0 = cb * CONV_ROWS
        acc = jnp.zeros((CONV_ROWS, D_CONV), f32) + bdw_ref[...]
        for j in range(CONV_WIDTH):
            s = (shift + j) % SUBLANES
            a = shift + j - s
            acc = acc + wdw_ref[j:j + 1, :] * ext_ref[s, r0 + a:r0 + a + CONV_ROWS, :]
        mu = jnp.mean(acc, axis=-1, keepdims=True)
        d = acc - mu
        var = jnp.mean(d * d, axis=-1, keepdims=True)
        y = d * lax.rsqrt(var + LN_EPS) * lng_ref[...] + lnb_ref[...]
        conv_ref[r0:r0 + CONV_ROWS, :] = (y * (1.0 / (1.0 + jnp.exp(-y)))).astype(conv_ref.dtype)
    ext_ref[0, 0:CONV_HALO, :] = ext_ref[0, tm:tm + CONV_HALO, :]

    f = jnp.dot(xb, wf_ref[...], preferred_element_type=f32) + bf_ref[...]
    log_f = jnp.minimum(f, 0.0) - jnp.log1p(jnp.exp(-jnp.abs(f)))
    lane = lax.broadcasted_iota(jnp.int32, log_f.shape, 1)
    log_f = jnp.where(((lane & (PART_STRIDE - 1)) < ATT_HEADS) & (lane < 3 * PART_STRIDE), log_f, 0.0)
    tri = tri_ref[...]
    hi, mid, lo = _split3(log_f)
    c = (jnp.dot(tri, hi, preferred_element_type=f32)
         + jnp.dot(tri, mid, preferred_element_type=f32)
         + jnp.dot(tri, lo, preferred_element_type=f32)) + carry_ref[...]
    carry_ref[...] = c[tm - 1:tm, :]
    c = c * LOG2_E

    c_hi = c.astype(bf16).astype(f32)
    c_mid = (c - c_hi).astype(bf16).astype(f32)
    c_lo = c - c_hi - c_mid
    parts = jnp.where(lane < PART_STRIDE, c_hi, jnp.where(lane < 2 * PART_STRIDE, c_mid, c_lo))
    parts = jnp.where(lane == ONES_LANE, 1.0, parts)

    qa = jnp.dot(xb, wq_ref[...], preferred_element_type=f32) * (LOG2_E / math.sqrt(HEAD_DIM))
    qa = (qa + jnp.dot(parts.astype(bf16), selq_ref[...], preferred_element_type=f32)).astype(bf16)
    for h in range(ATT_HEADS):
        qa_ref[0, h] = qa[:, h * LANES:(h + 1) * LANES]

    kta = lax.dot_general(wkt_ref[...], xb, (((1,), (1,)), ((), ())), preferred_element_type=f32)
    kta = (kta + jnp.dot(selk_ref[...], parts.T.astype(bf16), preferred_element_type=f32)).astype(bf16)
    for h in range(ATT_HEADS):
        kta_ref[0, h, 0] = kta[h * LANES:(h + 1) * LANES, :]

    v_ref[...] = jnp.dot(xb, wv_ref[...], preferred_element_type=f32).astype(bf16)


def _inproj(x2d, norm_mix_g, w_in, b_f, b_glu, w_dw, b_dw, ln_g, ln_b, *, batch, seq):
    n = batch * seq
    tm = ROW_TILE
    nt = seq // tm
    wq = w_in[:, :D_ATT]
    wk = w_in[:, D_ATT:2 * D_ATT]
    wv = w_in[:, 2 * D_ATT:3 * D_ATT]
    wf = w_in[:, 3 * D_ATT:3 * D_ATT + ATT_HEADS]
    wglu = w_in[:, 3 * D_ATT + ATT_HEADS:]
    pad = LANES - HEAD_DIM
    wq_pad = jnp.pad(wq.reshape(D_MODEL, ATT_HEADS, HEAD_DIM), ((0, 0), (0, 0), (0, pad)))
    wq_pad = wq_pad.reshape(D_MODEL, ATT_HEADS * LANES).astype(bf16)
    wkt_pad = jnp.pad(wk.T.reshape(ATT_HEADS, HEAD_DIM, D_MODEL), ((0, 0), (0, pad), (0, 0)))
    wkt_pad = wkt_pad.reshape(ATT_HEADS * LANES, D_MODEL).astype(bf16)
    spread = lambda a: jnp.pad(jnp.concatenate(
        [jnp.pad(a, ((0, 0), (0, PART_STRIDE - ATT_HEADS)))] * 3, axis=1),
        ((0, 0), (0, LANES - 3 * PART_STRIDE)))
    wf_pad = spread(wf).astype(bf16)
    bf_pad = spread(b_f.reshape(1, ATT_HEADS))
    sel_q, sel_k = _selection_matrices()
    tri = jnp.asarray(np.tril(np.ones((tm, tm), np.float32)), bf16)

    const = lambda shape: pl.BlockSpec(shape, lambda i: (0,) * len(shape))
    return pl.pallas_call(
        functools.partial(_inproj_kernel, tiles_per_seq=nt),
        grid=(n // tm,),
        in_specs=[
            pl.BlockSpec((tm, D_MODEL), lambda i: (i, 0)),
            const((1, D_MODEL)),
            const((D_MODEL, ATT_HEADS * LANES)),
            const((ATT_HEADS * LANES, D_MODEL)),
            const((D_MODEL, D_ATT)),
            const((D_MODEL, LANES)),
            const((D_MODEL, 2 * D_CONV)),
            const((1, LANES)),
            const((1, 2 * D_CONV)),
            const((LANES, ATT_HEADS * LANES)),
            const((ATT_HEADS * LANES, LANES)),
            const((tm, tm)),
            const((CONV_HALO, D_CONV)), const((1, D_CONV)), const((1, D_CONV)), const((1, D_CONV)),
        ],
        out_specs=[
            pl.BlockSpec((1, ATT_HEADS, tm, LANES), lambda i: (i // nt, 0, i % nt, 0)),
            pl.BlockSpec((1, ATT_HEADS, 1, LANES, tm), lambda i: (i // nt, 0, i % nt, 0, 0)),
            pl.BlockSpec((tm, D_ATT), lambda i: (i, 0)),
            pl.BlockSpec((tm, D_CONV), lambda i: (i, 0)),
        ],
        out_shape=[
            jax.ShapeDtypeStruct((batch, ATT_HEADS, seq, LANES), bf16),
            jax.ShapeDtypeStruct((batch, ATT_HEADS, nt, LANES, tm), bf16),
            jax.ShapeDtypeStruct((n, D_ATT), bf16),
            jax.ShapeDtypeStruct((n, D_CONV), bf16),
        ],
        scratch_shapes=[pltpu.VMEM((1, LANES), f32),
                        pltpu.VMEM((SUBLANES, tm + CONV_HALO, D_CONV), f32)],
        compiler_params=pltpu.CompilerParams(
            dimension_semantics=("arbitrary",), vmem_limit_bytes=VMEM_LIMIT),
        name="inproj",
    )(x2d, norm_mix_g.reshape(1, D_MODEL), wq_pad, wkt_pad, wv.astype(bf16), wf_pad,
      wglu.astype(bf16), bf_pad, b_glu.reshape(1, 2 * D_CONV), sel_q, sel_k, tri,
      jnp.pad(w_dw, ((0, CONV_HALO - CONV_WIDTH), (0, 0))), b_dw.reshape(1, D_CONV),
      ln_g.reshape(1, D_CONV), ln_b.reshape(1, D_CONV))


def _attn_kernel(qa_ref, kta_ref, v_ref, o_ref, acc_ref, m_ref, *, tile):
    i = pl.program_id(2)
    q_rows = ATT_Q_TILES * tile
    first_half = lax.broadcasted_iota(jnp.int32, (tile, LANES), 1) < HEAD_DIM
    acc_ref[...] = jnp.zeros_like(acc_ref)
    m_ref[...] = jnp.full_like(m_ref, MASK_VALUE)
    sub = ATT_SUB
    blocks_per_tile = tile // sub

    def process(kv_tiles):
        v_augs = []
        for j, _ in kv_tiles:
            vp = v_ref[0, pl.ds(pl.multiple_of(j * tile, tile), tile), :]
            one = jnp.ones_like(vp)
            v_augs.append((jnp.where(first_half, vp, one), jnp.where(first_half, one, vp)))
        chains = [(t, h, rb) for t in range(len(kv_tiles)) for h in range(2)
                  for rb in range(q_rows // sub) if kv_tiles[t][1][rb // blocks_per_tile] is not None]

        def visible_width(t, rb):
            if kv_tiles[t][1][rb // blocks_per_tile] == "diag":
                return (rb % blocks_per_tile + 1) * sub
            return tile

        def scores(t, h, rb):
            return jnp.dot(qa_ref[0, h, rb * sub:(rb + 1) * sub, :],
                           kta_ref[0, h, kv_tiles[t][0]][:, :visible_width(t, rb)],
                           preferred_element_type=f32)

        s_next = scores(*chains[0])
        for ci, (t, h, rb) in enumerate(chains):
            s = s_next
            if ci + 1 < len(chains):
                s_next = scores(*chains[ci + 1])
            rows = slice(rb * sub, (rb + 1) * sub)
            width = s.shape[1]
            if kv_tiles[t][1][rb // blocks_per_tile] == "diag":
                r = lax.broadcasted_iota(jnp.int32, s.shape, 0) + (rb % blocks_per_tile) * sub
                c = lax.broadcasted_iota(jnp.int32, s.shape, 1)
                s = jnp.where(c <= r, s, MASK_VALUE)
            m_old = m_ref[h, rows, :]
            m_new = jnp.maximum(m_old, jnp.max(s, axis=-1, keepdims=True))
            alpha = jnp.exp2(m_old - m_new)
            p = jnp.exp2(s - jnp.concatenate([m_new] * (width // LANES), axis=1)).astype(bf16)
            acc_ref[h, rows, :] = alpha * acc_ref[h, rows, :] + jnp.dot(
                p, v_augs[t][h][:width, :], preferred_element_type=f32)
            m_ref[h, rows, :] = m_new

    everything = ("full",) * ATT_Q_TILES

    def body(jj, carry):
        process([(ATT_Q_TILES * jj + t, everything) for t in range(ATT_Q_TILES)])
        return carry

    lax.fori_loop(0, i, body, 0)
    process([(ATT_Q_TILES * i + t,
              tuple("full" if t < qt else ("diag" if t == qt else None) for qt in range(ATT_Q_TILES)))
             for t in range(ATT_Q_TILES)])

    for qt in range(ATT_Q_TILES):
        a0 = acc_ref[0, qt * tile:(qt + 1) * tile, :]
        a1 = acc_ref[1, qt * tile:(qt + 1) * tile, :]
        l0 = a0[:, HEAD_DIM:HEAD_DIM + 1]
        l1 = a1[:, 0:1]
        o_ref[0, qt * tile:(qt + 1) * tile, :] = jnp.where(
            first_half, a0 / l0, a1 / l1).astype(o_ref.dtype)


def _attention(qa, kta, v3, *, batch, seq):
    tile = ROW_TILE
    nt = seq // tile
    q_rows = ATT_Q_TILES * tile
    return pl.pallas_call(
        functools.partial(_attn_kernel, tile=tile),
        grid=(batch, ATT_HEADS // 2, seq // q_rows),
        in_specs=[
            pl.BlockSpec((1, 2, q_rows, LANES), lambda b, hp, i: (b, hp, i, 0)),
            pl.BlockSpec((1, 2, nt, LANES, tile), lambda b, hp, i: (b, hp, 0, 0, 0)),
            pl.BlockSpec((1, seq, LANES), lambda b, hp, i: (b, 0, hp)),
        ],
        out_specs=pl.BlockSpec((1, q_rows, LANES), lambda b, hp, i: (b, i, hp)),
        out_shape=jax.ShapeDtypeStruct((batch, seq, D_ATT), bf16),
        scratch_shapes=[pltpu.VMEM((2, q_rows, LANES), f32), pltpu.VMEM((2, q_rows, LANES), f32)],
        compiler_params=pltpu.CompilerParams(
            dimension_semantics=("arbitrary", "arbitrary", "arbitrary"),
            vmem_limit_bytes=VMEM_LIMIT),
        name="attention",
    )(qa, kta, v3)


def _outproj_router_kernel(att_ref, conv_ref, x_ref, wo_ref, g_ref, wr_hi_ref, wr_lo_ref, br_ref,
                           triu_ref, tril_ref, h1_ref, hna_ref, dl_ref, col_ref, cnt_ref):
    h1 = (x_ref[...]
          + jnp.dot(att_ref[...], wo_ref[:D_ATT, :], preferred_element_type=f32)
          + jnp.dot(conv_ref[...], wo_ref[D_ATT:, :], preferred_element_type=f32))
    h1_ref[...] = h1
    hn = h1 * lax.rsqrt(jnp.mean(h1 * h1, axis=-1, keepdims=True) + RMS_EPS) * g_ref[...]

    hn_hi = hn.astype(bf16)
    hn_lo = (hn - hn_hi.astype(f32)).astype(bf16)
    nt = (((1,), (1,)), ((), ()))
    logits = (lax.dot_general(wr_hi_ref[...], hn_hi, nt, preferred_element_type=f32)
              + lax.dot_general(wr_hi_ref[...], hn_lo, nt, preferred_element_type=f32)
              + lax.dot_general(wr_lo_ref[...], hn_hi, nt, preferred_element_type=f32)
              + br_ref[...])

    erow = lax.broadcasted_iota(jnp.int32, logits.shape, 0)
    work = logits
    vals, idxs = [], []
    for _ in range(TOP_K):
        mk = jnp.max(work, axis=0, keepdims=True)
        ik = jnp.min(jnp.where(work == mk, erow, N_EXPERTS), axis=0, keepdims=True)
        work = jnp.where(erow == ik, -jnp.inf, work)
        vals.append(mk)
        idxs.append(ik)
    exps = [jnp.exp(v - vals[0]) for v in vals]
    denom = exps[0] + exps[1] + exps[2] + exps[3]

    onehot = jnp.zeros(logits.shape, f32)
    gates = jnp.zeros(logits.shape, f32)
    for k in range(TOP_K):
        hit = erow == idxs[k]
        onehot = onehot + jnp.where(hit, 1.0, 0.0)
        gates = gates + jnp.where(hit, exps[k] / denom, 0.0)

    before = jnp.dot(onehot.astype(bf16), triu_ref[...], preferred_element_type=f32)
    count = jnp.sum(onehot, axis=1, keepdims=True)
    slots = jnp.floor((count + (SLOT_ROWS - 1)) * (1.0 / SLOT_ROWS))
    slots_b = jnp.broadcast_to(slots, (N_EXPERTS, LANES))
    slot_start = jnp.dot(tril_ref[...], slots_b.astype(bf16), preferred_element_type=f32)
    cnt_ref[0] = slots_b
    pos = before + slot_start[:, 0:1] * SLOT_ROWS
    dls = []
    for k in range(TOP_K):
        dk = jnp.sum(jnp.where(erow == idxs[k], pos, 0.0), axis=0, keepdims=True)
        dl_ref[k:k + 1, :] = dk.astype(jnp.int32)
        dls.append(dk)

    g_hi, g_mid, g_lo = _split3(gates)
    tm = logits.shape[1]
    r8 = lax.broadcasted_iota(jnp.int32, (8, tm), 0)
    dl8 = jnp.zeros((8, tm), f32)
    for k in range(TOP_K):
        dl8 = jnp.where(r8 == k, dls[k], dl8)
    stack = jnp.concatenate(
        [g_hi.astype(f32), g_mid.astype(f32), g_lo.astype(f32), dl8,
         jnp.zeros((LANES - 3 * N_EXPERTS - 8, tm), f32)], axis=0)
    col = stack.T
    col_ref[...] = col
    lane = lax.broadcasted_iota(jnp.int32, col.shape, 1)
    hna_ref[:, :D_MODEL] = hn_hi
    hna_ref[:, D_MODEL:] = jnp.where(lane < 3 * N_EXPERTS, col, 0.0).astype(bf16)


def _outproj_router(att2d, conv2d, x2d, w_out, norm_ffn_g, w_router, b_router):
    n = x2d.shape[0]
    tm = ROW_TILE
    wr_t = w_router.T
    wr_hi = wr_t.astype(bf16)
    wr_lo = (wr_t - wr_hi.astype(f32)).astype(bf16)
    triu = jnp.asarray(np.triu(np.ones((tm, tm), np.float32), k=1), bf16)
    tril = jnp.asarray(np.tril(np.ones((N_EXPERTS, N_EXPERTS), np.float32), k=-1), bf16)
    const = lambda shape: pl.BlockSpec(shape, lambda i: (0,) * len(shape))
    rows = lambda width: pl.BlockSpec((tm, width), lambda i: (i, 0))
    return pl.pallas_call(
        _outproj_router_kernel,
        grid=(n // tm,),
        in_specs=[
            rows(D_ATT), rows(D_CONV), rows(D_MODEL),
            const((D_MODEL, D_MODEL)), const((1, D_MODEL)),
            const((N_EXPERTS, D_MODEL)), const((N_EXPERTS, D_MODEL)), const((N_EXPERTS, 1)),
            const((tm, tm)), const((N_EXPERTS, N_EXPERTS)),
        ],
        out_specs=[
            rows(D_MODEL), rows(X_WIDTH),
            pl.BlockSpec((TOP_K, tm), lambda i: (0, i)),
            rows(LANES),
            pl.BlockSpec((1, N_EXPERTS, LANES), lambda i: (i, 0, 0)),
        ],
        out_shape=[
            jax.ShapeDtypeStruct((n, D_MODEL), f32),
            jax.ShapeDtypeStruct((n, X_WIDTH), bf16),
            jax.ShapeDtypeStruct((TOP_K, n), jnp.int32),
            jax.ShapeDtypeStruct((n, LANES), f32),
            jax.ShapeDtypeStruct((n // tm, N_EXPERTS, LANES), f32),
        ],
        compiler_params=pltpu.CompilerParams(
            dimension_semantics=("arbitrary",), vmem_limit_bytes=VMEM_LIMIT),
        name="outproj_router",
    )(att2d, conv2d, x2d, w_out.astype(bf16), norm_ffn_g.reshape(1, D_MODEL), wr_hi, wr_lo,
      b_router.reshape(N_EXPERTS, 1), triu, tril)


def _dispatch_kernel(pdest_ref, npiece_ref, padd_ref, npad_ref, nused_ref, dl_ref, hna_ref, xs_ref,
                     xbuf, zbuf, sems, zsem, *, n_tiles, n_group_tiles):
    i = pl.program_id(0)
    slot = i % 2

    def piece_copy(t, s, j):
        dst = pl.multiple_of(pdest_ref[t * PIECES_PER_TILE + j], SLOT_ROWS)
        return pltpu.make_async_copy(
            xbuf.at[s, pl.ds(pl.multiple_of(j * SLOT_ROWS, SLOT_ROWS), SLOT_ROWS)],
            xs_ref.at[pl.ds(dst, SLOT_ROWS)], sems.at[s])

    def for_each_piece(t, s, action):
        for j in range(MIN_PIECES):
            action(piece_copy(t, s, j))

        def rest(j, c):
            action(piece_copy(t, s, j))
            return c
        lax.fori_loop(MIN_PIECES, npiece_ref[t], rest, 0)

    def pad_copy(m):
        dst = pl.multiple_of(padd_ref[m], SLOT_ROWS)
        return pltpu.make_async_copy(
            zbuf.at[pl.ds(0, SLOT_ROWS)], xs_ref.at[pl.ds(dst, SLOT_ROWS)], zsem)

    def tail_copy(t):
        dst = pl.multiple_of(t * GROUP_TILE, GROUP_TILE)
        return pltpu.make_async_copy(zbuf, xs_ref.at[pl.ds(dst, GROUP_TILE)], zsem)

    @pl.when(i >= 2)
    def _():
        for_each_piece(jnp.maximum(i - 2, 0), slot, lambda c: c.wait())

    @pl.when(i < n_tiles)
    def _():
        dl = dl_ref[...]
        rhs = hna_ref[...]
        for c in range(TILE_SLOT_ROWS // PERM_CHUNK):
            r = lax.broadcasted_iota(jnp.int32, (PERM_CHUNK, dl.shape[1]), 0) + c * PERM_CHUNK
            p = jnp.zeros(r.shape, f32)
            for k in range(TOP_K):
                p = jnp.where(dl[k:k + 1, :] == r, 1.0, p)
            xbuf[slot, c * PERM_CHUNK:(c + 1) * PERM_CHUNK, :] = jnp.dot(
                p.astype(bf16), rhs, preferred_element_type=f32).astype(bf16)

        for_each_piece(jnp.minimum(i, n_tiles - 1), slot, lambda c: c.start())

    @pl.when(i == n_tiles)
    def _():
        zbuf[...] = jnp.zeros_like(zbuf)

        def start(m, c):
            pad_copy(m).start()
            return c
        lax.fori_loop(0, npad_ref[0], start, 0)

        def start_tail(t, c):
            tail_copy(t).start()
            return c
        lax.fori_loop(nused_ref[0], n_group_tiles, start_tail, 0)

    @pl.when(i == n_tiles + 1)
    def _():
        def wait(m, c):
            pad_copy(m).wait()
            return c
        lax.fori_loop(0, npad_ref[0], wait, 0)

        def wait_tail(t, c):
            tail_copy(t).wait()
            return c
        lax.fori_loop(nused_ref[0], n_group_tiles, wait_tail, 0)


def _dispatch(hna, dl, plan, n_rows):
    n = hna.shape[0]
    tm = ROW_TILE
    nt = n // tm
    last = nt - 1
    grid_spec = pltpu.PrefetchScalarGridSpec(
        num_scalar_prefetch=5,
        grid=(nt + 2,),
        in_specs=[
            pl.BlockSpec((TOP_K, tm), lambda i, *_: (0, jnp.minimum(i, last))),
            pl.BlockSpec((tm, X_WIDTH), lambda i, *_: (jnp.minimum(i, last), 0)),
        ],
        out_specs=pl.BlockSpec(memory_space=pl.ANY),
        scratch_shapes=[
            pltpu.VMEM((2, TILE_SLOT_ROWS, X_WIDTH), bf16),
            pltpu.VMEM((GROUP_TILE, X_WIDTH), bf16),
            pltpu.SemaphoreType.DMA((2,)),
            pltpu.SemaphoreType.DMA,
        ],
    )
    return pl.pallas_call(
        functools.partial(_dispatch_kernel, n_tiles=nt, n_group_tiles=n_rows // GROUP_TILE),
        grid_spec=grid_spec,
        out_shape=jax.ShapeDtypeStruct((n_rows, X_WIDTH), bf16),
        compiler_params=pltpu.CompilerParams(
            dimension_semantics=("arbitrary",), vmem_limit_bytes=VMEM_LIMIT),
        name="dispatch",
    )(plan["piece_dest"], plan["n_pieces"], plan["pad_dest"], plan["n_pad"], plan["n_used"], dl, hna)


def _moe_kernel(rstart_ref, rtiles_ref, nused_ref, xs_ref, wgu_hbm, bgu_ref, wd_hbm, bd_ref, ys_ref,
                wgu_f32, wd_f32, wgu_bf, wd_bf, xbuf, ybuf, wsem, xsem, ysem, *, n_group_tiles):
    e = pl.program_id(0)
    n_t = rtiles_ref[e]
    row0 = rstart_ref[e]
    ws = e % 2

    def w_copies(ex, slot):
        return (pltpu.make_async_copy(wgu_hbm.at[ex], wgu_f32.at[slot], wsem.at[slot]),
                pltpu.make_async_copy(wd_hbm.at[ex], wd_f32.at[slot], wsem.at[slot]))

    def x_copy(t, s):
        src = pl.multiple_of(row0 + t * GROUP_TILE, GROUP_TILE)
        return pltpu.make_async_copy(xs_ref.at[pl.ds(src, GROUP_TILE)], xbuf.at[s], xsem.at[s])

    def y_copy(t, s):
        dst = pl.multiple_of(row0 + t * GROUP_TILE, GROUP_TILE)
        return pltpu.make_async_copy(ybuf.at[s], ys_ref.at[pl.ds(dst, GROUP_TILE)], ysem.at[s])

    def tail_copy(t):
        dst = pl.multiple_of(t * GROUP_TILE, GROUP_TILE)
        return pltpu.make_async_copy(ybuf.at[0], ys_ref.at[pl.ds(dst, GROUP_TILE)], ysem.at[0])

    @pl.when(e == 0)
    def _():
        for c in w_copies(0, 0):
            c.start()

    @pl.when(n_t > 0)
    def _():
        x_copy(0, 0).start()

    @pl.when(n_t > 1)
    def _():
        x_copy(1, 1).start()

    @pl.when(e + 1 < N_EXPERTS)
    def _():
        for c in w_copies(jnp.minimum(e + 1, N_EXPERTS - 1), 1 - ws):
            c.start()

    for c in w_copies(e, ws):
        c.wait()

    @pl.when(n_t > 0)
    def _():
        wgu_bf[...] = wgu_f32[ws].astype(bf16)
        wd_bf[...] = wd_f32[ws].astype(bf16)

        def body(t, carry):
            s = t % 2
            x_copy(t, s).wait()

            @pl.when(t >= 2)
            def _():
                y_copy(t - 2, s).wait()

            h = jnp.dot(xbuf[s, :, :D_MODEL], wgu_bf[...], preferred_element_type=f32) + bgu_ref[0]
            hg = jnp.minimum(h[:, :D_FF], SWIGLU_LIMIT)
            hu = jnp.clip(h[:, D_FF:], -SWIGLU_LIMIT, SWIGLU_LIMIT)
            act = (hu + 1.0) * (hg * (1.0 / (1.0 + jnp.exp(-SWIGLU_ALPHA * hg))))
            y = jnp.dot(act.astype(bf16), wd_bf[...], preferred_element_type=f32) + bd_ref[0]
            aug = xbuf[s, :, D_MODEL:].astype(f32)
            lane = lax.broadcasted_iota(jnp.int32, aug.shape, 1)
            gate = jnp.sum(jnp.where((lane & (N_EXPERTS - 1)) == e, aug, 0.0), axis=1, keepdims=True)
            ybuf[s] = (gate * y).astype(ybuf.dtype)
            y_copy(t, s).start()

            @pl.when(t + 2 < n_t)
            def _():
                x_copy(t + 2, s).start()

            return carry

        lax.fori_loop(0, n_t, body, 0)

        @pl.when(n_t >= 2)
        def _():
            y_copy(n_t - 2, n_t % 2).wait()

        y_copy(n_t - 1, (n_t - 1) % 2).wait()

    @pl.when(e == N_EXPERTS - 1)
    def _():
        ybuf[0] = jnp.zeros(ybuf.shape[1:], ybuf.dtype)

        def start_tail(t, carry):
            tail_copy(t).start()
            return carry

        def wait_tail(t, carry):
            tail_copy(t).wait()
            return carry

        lax.fori_loop(nused_ref[0], n_group_tiles, start_tail, 0)
        lax.fori_loop(nused_ref[0], n_group_tiles, wait_tail, 0)


def _moe(xs, plan, w_gu, b_gu, w_down, b_down):
    n_rows = xs.shape[0]
    tg = GROUP_TILE
    grid_spec = pltpu.PrefetchScalarGridSpec(
        num_scalar_prefetch=3,
        grid=(N_EXPERTS,),
        in_specs=[
            pl.BlockSpec(memory_space=pl.ANY),
            pl.BlockSpec(memory_space=pl.ANY),
            pl.BlockSpec((1, 1, 2 * D_FF), lambda e, *_: (e, 0, 0)),
            pl.BlockSpec(memory_space=pl.ANY),
            pl.BlockSpec((1, 1, D_MODEL), lambda e, *_: (e, 0, 0)),
        ],
        out_specs=pl.BlockSpec(memory_space=pl.ANY),
        scratch_shapes=[
            pltpu.VMEM((2, D_MODEL, 2 * D_FF), f32),
            pltpu.VMEM((2, D_FF, D_MODEL), f32),
            pltpu.VMEM((D_MODEL, 2 * D_FF), bf16),
            pltpu.VMEM((D_FF, D_MODEL), bf16),
            pltpu.VMEM((2, tg, X_WIDTH), bf16),
            pltpu.VMEM((2, tg, D_MODEL), bf16),
            pltpu.SemaphoreType.DMA((2,)),
            pltpu.SemaphoreType.DMA((2,)),
            pltpu.SemaphoreType.DMA((2,)),
        ],
    )
    return pl.pallas_call(
        functools.partial(_moe_kernel, n_group_tiles=n_rows // tg),
        grid_spec=grid_spec,
        out_shape=jax.ShapeDtypeStruct((n_rows, D_MODEL), bf16),
        compiler_params=pltpu.CompilerParams(
            dimension_semantics=("arbitrary",), vmem_limit_bytes=VMEM_LIMIT),
        name="moe_experts",
    )(plan["region_start"], plan["region_tiles"], plan["n_used"], xs, w_gu,
      b_gu.reshape(N_EXPERTS, 1, 2 * D_FF), w_down, b_down.reshape(N_EXPERTS, 1, D_MODEL))


def _combine_kernel(pdest_ref, npiece_ref, h1_ref, col_ref, g_ref, ys_ref, o_ref, ybuf, sems,
                    *, n_tiles):
    i = pl.program_id(0)
    slot = i % 2

    def piece_copy(t, s, j):
        src = pl.multiple_of(pdest_ref[t * PIECES_PER_TILE + j], SLOT_ROWS)
        return pltpu.make_async_copy(
            ys_ref.at[pl.ds(src, SLOT_ROWS)],
            ybuf.at[s, pl.ds(pl.multiple_of(j * SLOT_ROWS, SLOT_ROWS), SLOT_ROWS)], sems.at[s])

    def for_each_piece(t, s, action):
        for j in range(MIN_PIECES):
            action(piece_copy(t, s, j))

        def rest(j, c):
            action(piece_copy(t, s, j))
            return c
        lax.fori_loop(MIN_PIECES, npiece_ref[t], rest, 0)

    def fetch(t, s):
        for_each_piece(t, s, lambda c: c.start())

    @pl.when(i == 0)
    def _():
        ybuf[...] = jnp.zeros_like(ybuf)
        fetch(0, 0)

    @pl.when(i + 1 < n_tiles)
    def _():
        fetch(jnp.minimum(i + 1, n_tiles - 1), 1 - slot)

    for_each_piece(i, slot, lambda c: c.wait())

    col = col_ref[...]
    rows = [col[:, 3 * N_EXPERTS + k:3 * N_EXPERTS + k + 1].astype(jnp.int32) for k in range(TOP_K)]
    h = h1_ref[...]
    for c in range(TILE_SLOT_ROWS // PERM_CHUNK):
        r = lax.broadcasted_iota(jnp.int32, (col.shape[0], PERM_CHUNK), 1) + c * PERM_CHUNK
        g = jnp.zeros(r.shape, f32)
        for k in range(TOP_K):
            g = jnp.where(rows[k] == r, 1.0, g)
        h = h + jnp.dot(g.astype(bf16), ybuf[slot, c * PERM_CHUNK:(c + 1) * PERM_CHUNK, :],
                        preferred_element_type=f32)
    o_ref[...] = h * lax.rsqrt(jnp.mean(h * h, axis=-1, keepdims=True) + RMS_EPS) * g_ref[...]


def _combine(h1, ys, col, plan, norm_final_g):
    n = h1.shape[0]
    tm = ROW_TILE
    nt = n // tm
    grid_spec = pltpu.PrefetchScalarGridSpec(
        num_scalar_prefetch=2,
        grid=(nt,),
        in_specs=[
            pl.BlockSpec((tm, D_MODEL), lambda i, *_: (i, 0)),
            pl.BlockSpec((tm, LANES), lambda i, *_: (i, 0)),
            pl.BlockSpec((1, D_MODEL), lambda i, *_: (0, 0)),
            pl.BlockSpec(memory_space=pl.ANY),
        ],
        out_specs=pl.BlockSpec((tm, D_MODEL), lambda i, *_: (i, 0)),
        scratch_shapes=[
            pltpu.VMEM((2, TILE_SLOT_ROWS, D_MODEL), bf16),
            pltpu.SemaphoreType.DMA((2,)),
        ],
    )
    return pl.pallas_call(
        functools.partial(_combine_kernel, n_tiles=nt),
        grid_spec=grid_spec,
        out_shape=jax.ShapeDtypeStruct((n, D_MODEL), f32),
        compiler_params=pltpu.CompilerParams(
            dimension_semantics=("arbitrary",), vmem_limit_bytes=VMEM_LIMIT),
        name="combine",
    )(plan["piece_dest"], plan["n_pieces"], h1, col, norm_final_g.reshape(1, D_MODEL), ys)


def _routing_plan(slot_counts):
    tg = GROUP_TILE
    i32 = jnp.int32
    pc = slot_counts.astype(i32) * SLOT_ROWS
    local_end = jnp.cumsum(pc, axis=1)
    local_start = local_end - pc
    total = jnp.sum(pc, axis=0)
    region = ((total + tg - 1) // tg) * tg
    region_end = jnp.cumsum(region)
    region_start = region_end - region
    base = region_start[None, :] + jnp.cumsum(pc, axis=0) - pc
    n_pieces = local_end[:, -1] // SLOT_ROWS

    piece_row = jnp.arange(PIECES_PER_TILE, dtype=i32) * SLOT_ROWS
    owner = jnp.sum((local_end[:, None, :] <= piece_row[None, :, None]).astype(i32), axis=2)
    owner_hit = owner[:, :, None] == jnp.arange(N_EXPERTS, dtype=i32)[None, None, :]
    shift = jnp.sum(jnp.where(owner_hit, (base - local_start)[:, None, :], 0), axis=2)
    piece_dest = shift + piece_row[None, :]
    piece_dest = jnp.where(piece_row[None, :] < local_end[:, -1:], piece_dest, 0)

    pad_slots = (region - total) // SLOT_ROWS
    pad_end = jnp.cumsum(pad_slots)
    m = jnp.arange(N_EXPERTS * (tg // SLOT_ROWS), dtype=i32)
    pad_owner = jnp.minimum(jnp.sum((pad_end[None, :] <= m[:, None]).astype(i32), axis=1),
                            N_EXPERTS - 1)
    pad_hit = pad_owner[:, None] == jnp.arange(N_EXPERTS, dtype=i32)[None, :]
    pad_first = jnp.sum(jnp.where(pad_hit, (region_start + total - (pad_end - pad_slots) * SLOT_ROWS)
                                  [None, :], 0), axis=1)
    pad_dest = jnp.where(m < pad_end[-1], pad_first + m * SLOT_ROWS, 0)

    return {
        "piece_dest": piece_dest.reshape(-1), "n_pieces": n_pieces, "pad_dest": pad_dest,
        "n_pad": pad_end[-1:],
        "region_start": region_start, "region_tiles": region // tg,
        "n_used": (region_end[-1] // tg).reshape(1),
    }


def kernel(x, norm_mix_g, w_in, b_f, b_glu, w_dw, b_dw, ln_g, ln_b, w_out, norm_ffn_g, w_router,
           b_router, w_gu, b_gu, w_down, b_down, norm_final_g):
    batch, seq, d = x.shape
    assert d == D_MODEL and seq % (ATT_Q_TILES * ROW_TILE) == 0, (batch, seq, d)
    n = batch * seq
    x2d = x.reshape(n, d)

    qa, kta, v, conv = _inproj(x2d, norm_mix_g, w_in, b_f, b_glu, w_dw, b_dw, ln_g, ln_b,
                               batch=batch, seq=seq)
    att = _attention(qa, kta, v.reshape(batch, seq, D_ATT), batch=batch, seq=seq)

    h1, hna, dl, col, slot_counts = _outproj_router(
        att.reshape(n, D_ATT), conv, x2d, w_out, norm_ffn_g, w_router, b_router)

    n_token_tiles = n // ROW_TILE
    max_rows = n * TOP_K + N_EXPERTS * (n_token_tiles * (SLOT_ROWS - 1) + GROUP_TILE - 1)
    n_group_tiles = -(-max_rows // GROUP_TILE)
    plan = _routing_plan(slot_counts[:, :, 0])
    xs = _dispatch(hna, dl, plan, n_group_tiles * GROUP_TILE)
    ys = _moe(xs, plan, w_gu, b_gu, w_down, b_down)
    out = _combine(h1, ys, col, plan, norm_final_g)
    return out.reshape(batch, seq, d)
```

```python
import functools
import math

import numpy as np
import jax
import jax.numpy as jnp
from jax import lax
from jax.experimental import pallas as pl
from jax.experimental.pallas import tpu as pltpu

D_MODEL = 1024
ATT_HEADS = 8
HEAD_DIM = 64
D_ATT = ATT_HEADS * HEAD_DIM
D_CONV = D_MODEL - D_ATT
CONV_WIDTH = 31
N_EXPERTS = 32
TOP_K = 4
D_FF = 1024
SWIGLU_LIMIT = 7.0
SWIGLU_ALPHA = 1.702
RMS_EPS = 1e-6
LN_EPS = 1e-5
MASK_VALUE = -1e30
LOG2_E = 1.4426950408889634

LANES = 128
SUBLANES = 8
CONV_ROWS = 64
ROW_TILE = 512
ATT_SUB = 512
ATT_Q_TILES = 2
CONV_HALO = 32
GROUP_TILE = 256
SLOT_ROWS = 16
X_WIDTH = D_MODEL + LANES
TILE_SLOT_ROWS = ROW_TILE * TOP_K + N_EXPERTS * SLOT_ROWS
PIECES_PER_TILE = TILE_SLOT_ROWS // SLOT_ROWS
MIN_PIECES = ROW_TILE * TOP_K // SLOT_ROWS
PERM_CHUNK = 512
VMEM_LIMIT = 56 * 1024 * 1024

AUG_Q = HEAD_DIM
AUG_K = HEAD_DIM + 3
ONES_LANE = ATT_HEADS
PART_STRIDE = 16

f32 = jnp.float32
bf16 = jnp.bfloat16


def _split3(x):
    hi = x.astype(bf16)
    r1 = x - hi.astype(f32)
    mid = r1.astype(bf16)
    lo = (r1 - mid.astype(f32)).astype(bf16)
    return hi, mid, lo


def _selection_matrices():
    sel_q = np.zeros((LANES, ATT_HEADS * LANES), np.float32)
    sel_k = np.zeros((ATT_HEADS * LANES, LANES), np.float32)
    for h in range(ATT_HEADS):
        base = h * LANES
        for p in range(3):
            sel_q[PART_STRIDE * p + h, base + AUG_Q + p] = 1.0
            sel_q[ONES_LANE, base + AUG_K + p] = 1.0
            sel_k[base + AUG_K + p, PART_STRIDE * p + h] = -1.0
            sel_k[base + AUG_Q + p, ONES_LANE] = 1.0
    return jnp.asarray(sel_q, bf16), jnp.asarray(sel_k, bf16)


def _inproj_kernel(x_ref, g_ref, wq_ref, wkt_ref, wv_ref, wf_ref, wglu_ref, bf_ref, bglu_ref,
                   selq_ref, selk_ref, tri_ref, wdw_ref, bdw_ref, lng_ref, lnb_ref,
                   qa_ref, kta_ref, v_ref, conv_ref, carry_ref, ext_ref, *, tiles_per_seq):
    i = pl.program_id(0)
    tm = x_ref.shape[0]

    @pl.when(i % tiles_per_seq == 0)
    def _():
        carry_ref[...] = jnp.zeros_like(carry_ref)
        ext_ref[0, 0:CONV_HALO, :] = jnp.zeros((CONV_HALO, D_CONV), f32)

    x = x_ref[...]
    xn = x * lax.rsqrt(jnp.mean(x * x, axis=-1, keepdims=True) + RMS_EPS) * g_ref[...]
    xb = xn.astype(bf16)

    glu = jnp.dot(xb, wglu_ref[...], preferred_element_type=f32) + bglu_ref[...]
    ext_ref[0, CONV_HALO:, :] = glu[:, :D_CONV] * (1.0 / (1.0 + jnp.exp(-glu[:, D_CONV:])))
    span = tm + CONV_HALO - SUBLANES
    for s in range(1, SUBLANES):
        ext_ref[s, 0:span, :] = ext_ref[0, s:s + span, :]
    shift = CONV_HALO - (CONV_WIDTH - 1)
    for cb in range(tm // CONV_ROWS):
        r0 = cb * CONV_ROWS
        acc = jnp.zeros((CONV_ROWS, D_CONV), f32) + bdw_ref[...]
        for j in range(CONV_WIDTH):
            s = (shift + j) % SUBLANES
            a = shift + j - s
            acc = acc + wdw_ref[j:j + 1, :] * ext_ref[s, r0 + a:r0 + a + CONV_ROWS, :]
        mu = jnp.mean(acc, axis=-1, keepdims=True)
        d = acc - mu
        var = jnp.mean(d * d, axis=-1, keepdims=True)
        y = d * lax.rsqrt(var + LN_EPS) * lng_ref[...] + lnb_ref[...]
        conv_ref[r0:r0 + CONV_ROWS, :] = (y * (1.0 / (1.0 + jnp.exp(-y)))).astype(conv_ref.dtype)
    ext_ref[0, 0:CONV_HALO, :] = ext_ref[0, tm:tm + CONV_HALO, :]

    f = jnp.dot(xb, wf_ref[...], preferred_element_type=f32) + bf_ref[...]
    log_f = jnp.minimum(f, 0.0) - jnp.log1p(jnp.exp(-jnp.abs(f)))
    lane = lax.broadcasted_iota(jnp.int32, log_f.shape, 1)
    log_f = jnp.where(((lane & (PART_STRIDE - 1)) < ATT_HEADS) & (lane < 3 * PART_STRIDE), log_f, 0.0)
    tri = tri_ref[...]
    hi, mid, lo = _split3(log_f)
    c = (jnp.dot(tri, hi, preferred_element_type=f32)
         + jnp.dot(tri, mid, preferred_element_type=f32)
         + jnp.dot(tri, lo, preferred_element_type=f32)) + carry_ref[...]
    carry_ref[...] = c[tm - 1:tm, :]
    c = c * LOG2_E

    c_hi = c.astype(bf16).astype(f32)
    c_mid = (c - c_hi).astype(bf16).astype(f32)
    c_lo = c - c_hi - c_mid
    parts = jnp.where(lane < PART_STRIDE, c_hi, jnp.where(lane < 2 * PART_STRIDE, c_mid, c_lo))
    parts = jnp.where(lane == ONES_LANE, 1.0, parts)

    qa = jnp.dot(xb, wq_ref[...], preferred_element_type=f32) * (LOG2_E / math.sqrt(HEAD_DIM))
    qa = (qa + jnp.dot(parts.astype(bf16), selq_ref[...], preferred_element_type=f32)).astype(bf16)
    for h in range(ATT_HEADS):
        qa_ref[0, h] = qa[:, h * LANES:(h + 1) * LANES]

    kta = lax.dot_general(wkt_ref[...], xb, (((1,), (1,)), ((), ())), preferred_element_type=f32)
    kta = (kta + jnp.dot(selk_ref[...], parts.T.astype(bf16), preferred_element_type=f32)).astype(bf16)
    for h in range(ATT_HEADS):
        kta_ref[0, h, 0] = kta[h * LANES:(h + 1) * LANES, :]

    v_ref[...] = jnp.dot(xb, wv_ref[...], preferred_element_type=f32).astype(bf16)


def _inproj(x2d, norm_mix_g, w_in, b_f, b_glu, w_dw, b_dw, ln_g, ln_b, *, batch, seq):
    n = batch * seq
    tm = ROW_TILE
    nt = seq // tm
    wq = w_in[:, :D_ATT]
    wk = w_in[:, D_ATT:2 * D_ATT]
    wv = w_in[:, 2 * D_ATT:3 * D_ATT]
    wf = w_in[:, 3 * D_ATT:3 * D_ATT + ATT_HEADS]
    wglu = w_in[:, 3 * D_ATT + ATT_HEADS:]
    pad = LANES - HEAD_DIM
    wq_pad = jnp.pad(wq.reshape(D_MODEL, ATT_HEADS, HEAD_DIM), ((0, 0), (0, 0), (0, pad)))
    wq_pad = wq_pad.reshape(D_MODEL, ATT_HEADS * LANES).astype(bf16)
    wkt_pad = jnp.pad(wk.T.reshape(ATT_HEADS, HEAD_DIM, D_MODEL), ((0, 0), (0, pad), (0, 0)))
    wkt_pad = wkt_pad.reshape(ATT_HEADS * LANES, D_MODEL).astype(bf16)
    spread = lambda a: jnp.pad(jnp.concatenate(
        [jnp.pad(a, ((0, 0), (0, PART_STRIDE - ATT_HEADS)))] * 3, axis=1),
        ((0, 0), (0, LANES - 3 * PART_STRIDE)))
    wf_pad = spread(wf).astype(bf16)
    bf_pad = spread(b_f.reshape(1, ATT_HEADS))
    sel_q, sel_k = _selection_matrices()
    tri = jnp.asarray(np.tril(np.ones((tm, tm), np.float32)), bf16)

    const = lambda shape: pl.BlockSpec(shape, lambda i: (0,) * len(shape))
    return pl.pallas_call(
        functools.partial(_inproj_kernel, tiles_per_seq=nt),
        grid=(n // tm,),
        in_specs=[
            pl.BlockSpec((tm, D_MODEL), lambda i: (i, 0)),
            const((1, D_MODEL)),
            const((D_MODEL, ATT_HEADS * LANES)),
            const((ATT_HEADS * LANES, D_MODEL)),
            const((D_MODEL, D_ATT)),
            const((D_MODEL, LANES)),
            const((D_MODEL, 2 * D_CONV)),
            const((1, LANES)),
            const((1, 2 * D_CONV)),
            const((LANES, ATT_HEADS * LANES)),
            const((ATT_HEADS * LANES, LANES)),
            const((tm, tm)),
            const((CONV_HALO, D_CONV)), const((1, D_CONV)), const((1, D_CONV)), const((1, D_CONV)),
        ],
        out_specs=[
            pl.BlockSpec((1, ATT_HEADS, tm, LANES), lambda i: (i // nt, 0, i % nt, 0)),
            pl.BlockSpec((1, ATT_HEADS, 1, LANES, tm), lambda i: (i // nt, 0, i % nt, 0, 0)),
            pl.BlockSpec((tm, D_ATT), lambda i: (i, 0)),
            pl.BlockSpec((tm, D_CONV), lambda i: (i, 0)),
        ],
        out_shape=[
            jax.ShapeDtypeStruct((batch, ATT_HEADS, seq, LANES), bf16),
            jax.ShapeDtypeStruct((batch, ATT_HEADS, nt, LANES, tm), bf16),
            jax.ShapeDtypeStruct((n, D_ATT), bf16),
            jax.ShapeDtypeStruct((n, D_CONV), bf16),
        ],
        scratch_shapes=[pltpu.VMEM((1, LANES), f32),
                        pltpu.VMEM((SUBLANES, tm + CONV_HALO, D_CONV), f32)],
        compiler_params=pltpu.CompilerParams(
            dimension_semantics=("arbitrary",), vmem_limit_bytes=VMEM_LIMIT),
        name="inproj",
    )(x2d, norm_mix_g.reshape(1, D_MODEL), wq_pad, wkt_pad, wv.astype(bf16), wf_pad,
      wglu.astype(bf16), bf_pad, b_glu.reshape(1, 2 * D_CONV), sel_q, sel_k, tri,
      jnp.pad(w_dw, ((0, CONV_HALO - CONV_WIDTH), (0, 0))), b_dw.reshape(1, D_CONV),
      ln_g.reshape(1, D_CONV), ln_b.reshape(1, D_CONV))


def _attn_kernel(qa_ref, kta_ref, v_ref, o_ref, acc_ref, m_ref, *, tile):
    i = pl.program_id(2)
    q_rows = ATT_Q_TILES * tile
    first_half = lax.broadcasted_iota(jnp.int32, (tile, LANES), 1) < HEAD_DIM
    acc_ref[...] = jnp.zeros_like(acc_ref)
    m_ref[...] = jnp.full_like(m_ref, MASK_VALUE)
    sub = ATT_SUB
    blocks_per_tile = tile // sub

    def process(kv_tiles):
        v_augs = []
        for j, _ in kv_tiles:
            vp = v_ref[0, pl.ds(pl.multiple_of(j * tile, tile), tile), :]
            one = jnp.ones_like(vp)
            v_augs.append((jnp.where(first_half, vp, one), jnp.where(first_half, one, vp)))
        chains = [(t, h, rb) for t in range(len(kv_tiles)) for h in range(2)
                  for rb in range(q_rows // sub) if kv_tiles[t][1][rb // blocks_per_tile] is not None]

        def visible_width(t, rb):
            if kv_tiles[t][1][rb // blocks_per_tile] == "diag":
                return (rb % blocks_per_tile + 1) * sub
            return tile

        def scores(t, h, rb):
            return jnp.dot(qa_ref[0, h, rb * sub:(rb + 1) * sub, :],
                           kta_ref[0, h, kv_tiles[t][0]][:, :visible_width(t, rb)],
                           preferred_element_type=f32)

        s_next = scores(*chains[0])
        for ci, (t, h, rb) in enumerate(chains):
            s = s_next
            if ci + 1 < len(chains):
                s_next = scores(*chains[ci + 1])
            rows = slice(rb * sub, (rb + 1) * sub)
            width = s.shape[1]
            if kv_tiles[t][1][rb // blocks_per_tile] == "diag":
                r = lax.broadcasted_iota(jnp.int32, s.shape, 0) + (rb % blocks_per_tile) * sub
                c = lax.broadcasted_iota(jnp.int32, s.shape, 1)
                s = jnp.where(c <= r, s, MASK_VALUE)
            m_old = m_ref[h, rows, :]
            m_new = jnp.maximum(m_old, jnp.max(s, axis=-1, keepdims=True))
            alpha = jnp.exp2(m_old - m_new)
            p = jnp.exp2(s - jnp.concatenate([m_new] * (width // LANES), axis=1)).astype(bf16)
            acc_ref[h, rows, :] = alpha * acc_ref[h, rows, :] + jnp.dot(
                p, v_augs[t][h][:width, :], preferred_element_type=f32)
            m_ref[h, rows, :] = m_new

    everything = ("full",) * ATT_Q_TILES

    def body(jj, carry):
        process([(ATT_Q_TILES * jj + t, everything) for t in range(ATT_Q_TILES)])
        return carry

    lax.fori_loop(0, i, body, 0)
    process([(ATT_Q_TILES * i + t,
              tuple("full" if t < qt else ("diag" if t == qt else None) for qt in range(ATT_Q_TILES)))
             for t in range(ATT_Q_TILES)])

    for qt in range(ATT_Q_TILES):
        a0 = acc_ref[0, qt * tile:(qt + 1) * tile, :]
        a1 = acc_ref[1, qt * tile:(qt + 1) * tile, :]
        l0 = a0[:, HEAD_DIM:HEAD_DIM + 1]
        l1 = a1[:, 0:1]
        o_ref[0, qt * tile:(qt + 1) * tile, :] = jnp.where(
            first_half, a0 / l0, a1 / l1).astype(o_ref.dtype)


def _attention(qa, kta, v3, *, batch, seq):
    tile = ROW_TILE
    nt = seq // tile
    q_rows = ATT_Q_TILES * tile
    return pl.pallas_call(
        functools.partial(_attn_kernel, tile=tile),
        grid=(batch, ATT_HEADS // 2, seq // q_rows),
        in_specs=[
            pl.BlockSpec((1, 2, q_rows, LANES), lambda b, hp, i: (b, hp, i, 0)),
            pl.BlockSpec((1, 2, nt, LANES, tile), lambda b, hp, i: (b, hp, 0, 0, 0)),
            pl.BlockSpec((1, seq, LANES), lambda b, hp, i: (b, 0, hp)),
        ],
        out_specs=pl.BlockSpec((1, q_rows, LANES), lambda b, hp, i: (b, i, hp)),
        out_shape=jax.ShapeDtypeStruct((batch, seq, D_ATT), bf16),
        scratch_shapes=[pltpu.VMEM((2, q_rows, LANES), f32), pltpu.VMEM((2, q_rows, LANES), f32)],
        compiler_params=pltpu.CompilerParams(
            dimension_semantics=("arbitrary", "arbitrary", "arbitrary"),
            vmem_limit_bytes=VMEM_LIMIT),
        name="attention",
    )(qa, kta, v3)


def _outproj_router_kernel(att_ref, conv_ref, x_ref, wo_ref, g_ref, wr_hi_ref, wr_lo_ref, br_ref,
                           triu_ref, tril_ref, h1_ref, hna_ref, dl_ref, col_ref, cnt_ref):
    h1 = (x_ref[...]
          + jnp.dot(att_ref[...], wo_ref[:D_ATT, :], preferred_element_type=f32)
          + jnp.dot(conv_ref[...], wo_ref[D_ATT:, :], preferred_element_type=f32))
    h1_ref[...] = h1
    hn = h1 * lax.rsqrt(jnp.mean(h1 * h1, axis=-1, keepdims=True) + RMS_EPS) * g_ref[...]

    hn_hi = hn.astype(bf16)
    hn_lo = (hn - hn_hi.astype(f32)).astype(bf16)
    nt = (((1,), (1,)), ((), ()))
    logits = (lax.dot_general(wr_hi_ref[...], hn_hi, nt, preferred_element_type=f32)
              + lax.dot_general(wr_hi_ref[...], hn_lo, nt, preferred_element_type=f32)
              + lax.dot_general(wr_lo_ref[...], hn_hi, nt, preferred_element_type=f32)
              + br_ref[...])

    erow = lax.broadcasted_iota(jnp.int32, logits.shape, 0)
    work = logits
    vals, idxs = [], []
    for _ in range(TOP_K):
        mk = jnp.max(work, axis=0, keepdims=True)
        ik = jnp.min(jnp.where(work == mk, erow, N_EXPERTS), axis=0, keepdims=True)
        work = jnp.where(erow == ik, -jnp.inf, work)
        vals.append(mk)
        idxs.append(ik)
    exps = [jnp.exp(v - vals[0]) for v in vals]
    denom = exps[0] + exps[1] + exps[2] + exps[3]

    onehot = jnp.zeros(logits.shape, f32)
    gates = jnp.zeros(logits.shape, f32)
    for k in range(TOP_K):
        hit = erow == idxs[k]
        onehot = onehot + jnp.where(hit, 1.0, 0.0)
        gates = gates + jnp.where(hit, exps[k] / denom, 0.0)

    before = jnp.dot(onehot.astype(bf16), triu_ref[...], preferred_element_type=f32)
    count = jnp.sum(onehot, axis=1, keepdims=True)
    slots = jnp.floor((count + (SLOT_ROWS - 1)) * (1.0 / SLOT_ROWS))
    slots_b = jnp.broadcast_to(slots, (N_EXPERTS, LANES))
    slot_start = jnp.dot(tril_ref[...], slots_b.astype(bf16), preferred_element_type=f32)
    cnt_ref[0] = slots_b
    pos = before + slot_start[:, 0:1] * SLOT_ROWS
    dls = []
    for k in range(TOP_K):
        dk = jnp.sum(jnp.where(erow == idxs[k], pos, 0.0), axis=0, keepdims=True)
        dl_ref[k:k + 1, :] = dk.astype(jnp.int32)
        dls.append(dk)

    g_hi, g_mid, g_lo = _split3(gates)
    tm = logits.shape[1]
    r8 = lax.broadcasted_iota(jnp.int32, (8, tm), 0)
    dl8 = jnp.zeros((8, tm), f32)
    for k in range(TOP_K):
        dl8 = jnp.where(r8 == k, dls[k], dl8)
    stack = jnp.concatenate(
        [g_hi.astype(f32), g_mid.astype(f32), g_lo.astype(f32), dl8,
         jnp.zeros((LANES - 3 * N_EXPERTS - 8, tm), f32)], axis=0)
    col = stack.T
    col_ref[...] = col
    lane = lax.broadcasted_iota(jnp.int32, col.shape, 1)
    hna_ref[:, :D_MODEL] = hn_hi
    hna_ref[:, D_MODEL:] = jnp.where(lane < 3 * N_EXPERTS, col, 0.0).astype(bf16)


def _outproj_router(att2d, conv2d, x2d, w_out, norm_ffn_g, w_router, b_router):
    n = x2d.shape[0]
    tm = ROW_TILE
    wr_t = w_router.T
    wr_hi = wr_t.astype(bf16)
    wr_lo = (wr_t - wr_hi.astype(f32)).astype(bf16)
    triu = jnp.asarray(np.triu(np.ones((tm, tm), np.float32), k=1), bf16)
    tril = jnp.asarray(np.tril(np.ones((N_EXPERTS, N_EXPERTS), np.float32), k=-1), bf16)
    const = lambda shape: pl.BlockSpec(shape, lambda i: (0,) * len(shape))
    rows = lambda width: pl.BlockSpec((tm, width), lambda i: (i, 0))
    return pl.pallas_call(
        _outproj_router_kernel,
        grid=(n // tm,),
        in_specs=[
            rows(D_ATT), rows(D_CONV), rows(D_MODEL),
            const((D_MODEL, D_MODEL)), const((1, D_MODEL)),
            const((N_EXPERTS, D_MODEL)), const((N_EXPERTS, D_MODEL)), const((N_EXPERTS, 1)),
            const((tm, tm)), const((N_EXPERTS, N_EXPERTS)),
        ],
        out_specs=[
            rows(D_MODEL), rows(X_WIDTH),
            pl.BlockSpec((TOP_K, tm), lambda i: (0, i)),
            rows(LANES),
            pl.BlockSpec((1, N_EXPERTS, LANES), lambda i: (i, 0, 0)),
        ],
        out_shape=[
            jax.ShapeDtypeStruct((n, D_MODEL), f32),
            jax.ShapeDtypeStruct((n, X_WIDTH), bf16),
            jax.ShapeDtypeStruct((TOP_K, n), jnp.int32),
            jax.ShapeDtypeStruct((n, LANES), f32),
            jax.ShapeDtypeStruct((n // tm, N_EXPERTS, LANES), f32),
        ],
        compiler_params=pltpu.CompilerParams(
            dimension_semantics=("arbitrary",), vmem_limit_bytes=VMEM_LIMIT),
        name="outproj_router",
    )(att2d, conv2d, x2d, w_out.astype(bf16), norm_ffn_g.reshape(1, D_MODEL), wr_hi, wr_lo,
      b_router.reshape(N_EXPERTS, 1), triu, tril)


def _dispatch_kernel(pdest_ref, npiece_ref, padd_ref, npad_ref, nused_ref, dl_ref, hna_ref, xs_ref,
                     xbuf, zbuf, sems, zsem, *, n_tiles, n_group_tiles):
    i = pl.program_id(0)
    slot = i % 2

    def piece_copy(t, s, j):
        dst = pl.multiple_of(pdest_ref[t * PIECES_PER_TILE + j], SLOT_ROWS)
        return pltpu.make_async_copy(
            xbuf.at[s, pl.ds(pl.multiple_of(j * SLOT_ROWS, SLOT_ROWS), SLOT_ROWS)],
            xs_ref.at[pl.ds(dst, SLOT_ROWS)], sems.at[s])

    def for_each_piece(t, s, action):
        for j in range(MIN_PIECES):
            action(piece_copy(t, s, j))

        def rest(j, c):
            action(piece_copy(t, s, j))
            return c
        lax.fori_loop(MIN_PIECES, npiece_ref[t], rest, 0)

    def pad_copy(m):
        dst = pl.multiple_of(padd_ref[m], SLOT_ROWS)
        return pltpu.make_async_copy(
            zbuf.at[pl.ds(0, SLOT_ROWS)], xs_ref.at[pl.ds(dst, SLOT_ROWS)], zsem)

    def tail_copy(t):
        dst = pl.multiple_of(t * GROUP_TILE, GROUP_TILE)
        return pltpu.make_async_copy(zbuf, xs_ref.at[pl.ds(dst, GROUP_TILE)], zsem)

    @pl.when(i >= 2)
    def _():
        for_each_piece(jnp.maximum(i - 2, 0), slot, lambda c: c.wait())

    @pl.when(i < n_tiles)
    def _():
        dl = dl_ref[...]
        rhs = hna_ref[...]
        for c in range(TILE_SLOT_ROWS // PERM_CHUNK):
            r = lax.broadcasted_iota(jnp.int32, (PERM_CHUNK, dl.shape[1]), 0) + c * PERM_CHUNK
            p = jnp.zeros(r.shape, f32)
            for k in range(TOP_K):
                p = jnp.where(dl[k:k + 1, :] == r, 1.0, p)
            xbuf[slot, c * PERM_CHUNK:(c + 1) * PERM_CHUNK, :] = jnp.dot(
                p.astype(bf16), rhs, preferred_element_type=f32).astype(bf16)

        for_each_piece(jnp.minimum(i, n_tiles - 1), slot, lambda c: c.start())

    @pl.when(i == n_tiles)
    def _():
        zbuf[...] = jnp.zeros_like(zbuf)

        def start(m, c):
            pad_copy(m).start()
            return c
        lax.fori_loop(0, npad_ref[0], start, 0)

        def start_tail(t, c):
            tail_copy(t).start()
            return c
        lax.fori_loop(nused_ref[0], n_group_tiles, start_tail, 0)

    @pl.when(i == n_tiles + 1)
    def _():
        def wait(m, c):
            pad_copy(m).wait()
            return c
        lax.fori_loop(0, npad_ref[0], wait, 0)

        def wait_tail(t, c):
            tail_copy(t).wait()
            return c
        lax.fori_loop(nused_ref[0], n_group_tiles, wait_tail, 0)


def _dispatch(hna, dl, plan, n_rows):
    n = hna.shape[0]
    tm = ROW_TILE
    nt = n // tm
    last = nt - 1
    grid_spec = pltpu.PrefetchScalarGridSpec(
        num_scalar_prefetch=5,
        grid=(nt + 2,),
        in_specs=[
            pl.BlockSpec((TOP_K, tm), lambda i, *_: (0, jnp.minimum(i, last))),
            pl.BlockSpec((tm, X_WIDTH), lambda i, *_: (jnp.minimum(i, last), 0)),
        ],
        out_specs=pl.BlockSpec(memory_space=pl.ANY),
        scratch_shapes=[
            pltpu.VMEM((2, TILE_SLOT_ROWS, X_WIDTH), bf16),
            pltpu.VMEM((GROUP_TILE, X_WIDTH), bf16),
            pltpu.SemaphoreType.DMA((2,)),
            pltpu.SemaphoreType.DMA,
        ],
    )
    return pl.pallas_call(
        functools.partial(_dispatch_kernel, n_tiles=nt, n_group_tiles=n_rows // GROUP_TILE),
        grid_spec=grid_spec,
        out_shape=jax.ShapeDtypeStruct((n_rows, X_WIDTH), bf16),
        compiler_params=pltpu.CompilerParams(
            dimension_semantics=("arbitrary",), vmem_limit_bytes=VMEM_LIMIT),
        name="dispatch",
    )(plan["piece_dest"], plan["n_pieces"], plan["pad_dest"], plan["n_pad"], plan["n_used"], dl, hna)


def _moe_kernel(rstart_ref, rtiles_ref, nused_ref, xs_ref, wgu_hbm, bgu_ref, wd_hbm, bd_ref, ys_ref,
                wgu_f32, wd_f32, wgu_bf, wd_bf, xbuf, ybuf, pend, wsem, xsem, ysem, *, n_group_tiles):
    e = pl.program_id(0)
    n_t = rtiles_ref[e]
    row0 = rstart_ref[e]
    ws = e % 2

    def w_copies(ex, slot):
        return (pltpu.make_async_copy(wgu_hbm.at[ex], wgu_f32.at[slot], wsem.at[slot]),
                pltpu.make_async_copy(wd_hbm.at[ex], wd_f32.at[slot], wsem.at[slot]))

    def x_copy(t, s):
        src = pl.multiple_of(row0 + t * GROUP_TILE, GROUP_TILE)
        return pltpu.make_async_copy(xs_ref.at[pl.ds(src, GROUP_TILE)], xbuf.at[s], xsem.at[s])

    def y_copy(t, s):
        dst = pl.multiple_of(row0 + t * GROUP_TILE, GROUP_TILE)
        return pltpu.make_async_copy(ybuf.at[s], ys_ref.at[pl.ds(dst, GROUP_TILE)], ysem.at[s])

    def tail_copy(t):
        dst = pl.multiple_of(t * GROUP_TILE, GROUP_TILE)
        return pltpu.make_async_copy(ybuf.at[0], ys_ref.at[pl.ds(dst, GROUP_TILE)], ysem.at[0])

    @pl.when(e == 0)
    def _():
        pend[0] = 0
        pend[1] = 0
        for c in w_copies(0, 0):
            c.start()

    @pl.when(n_t > 0)
    def _():
        x_copy(0, 0).start()

    @pl.when(n_t > 1)
    def _():
        x_copy(1, 1).start()

    @pl.when(e + 1 < N_EXPERTS)
    def _():
        for c in w_copies(jnp.minimum(e + 1, N_EXPERTS - 1), 1 - ws):
            c.start()

    for c in w_copies(e, ws):
        c.wait()

    @pl.when(n_t > 0)
    def _():
        wgu_bf[...] = wgu_f32[ws].astype(bf16)
        wd_bf[...] = wd_f32[ws].astype(bf16)

        def body(t, carry):
            s = t % 2
            x_copy(t, s).wait()

            @pl.when((t >= 2) | (pend[s] == 1))
            def _():
                y_copy(t, s).wait()
                pend[s] = 0

            h = jnp.dot(xbuf[s, :, :D_MODEL], wgu_bf[...], preferred_element_type=f32) + bgu_ref[0]
            hg = jnp.minimum(h[:, :D_FF], SWIGLU_LIMIT)
            hu = jnp.clip(h[:, D_FF:], -SWIGLU_LIMIT, SWIGLU_LIMIT)
            act = (hu + 1.0) * (hg * (1.0 / (1.0 + jnp.exp(-SWIGLU_ALPHA * hg))))
            y = jnp.dot(act.astype(bf16), wd_bf[...], preferred_element_type=f32) + bd_ref[0]
            aug = xbuf[s, :, D_MODEL:].astype(f32)
            lane = lax.broadcasted_iota(jnp.int32, aug.shape, 1)
            gate = jnp.sum(jnp.where((lane & (N_EXPERTS - 1)) == e, aug, 0.0), axis=1, keepdims=True)
            ybuf[s] = (gate * y).astype(ybuf.dtype)
            y_copy(t, s).start()

            @pl.when(t + 2 < n_t)
            def _():
                x_copy(t + 2, s).start()

            return carry

        lax.fori_loop(0, n_t, body, 0)
        pend[(n_t - 1) % 2] = 1

        @pl.when(n_t >= 2)
        def _():
            pend[n_t % 2] = 1

    @pl.when(e == N_EXPERTS - 1)
    def _():
        for s in range(2):
            @pl.when(pend[s] == 1)
            def _():
                y_copy(0, s).wait()
                pend[s] = 0

        ybuf[0] = jnp.zeros(ybuf.shape[1:], ybuf.dtype)

        def start_tail(t, carry):
            tail_copy(t).start()
            return carry

        def wait_tail(t, carry):
            tail_copy(t).wait()
            return carry

        lax.fori_loop(nused_ref[0], n_group_tiles, start_tail, 0)
        lax.fori_loop(nused_ref[0], n_group_tiles, wait_tail, 0)


def _moe(xs, plan, w_gu, b_gu, w_down, b_down):
    n_rows = xs.shape[0]
    tg = GROUP_TILE
    grid_spec = pltpu.PrefetchScalarGridSpec(
        num_scalar_prefetch=3,
        grid=(N_EXPERTS,),
        in_specs=[
            pl.BlockSpec(memory_space=pl.ANY),
            pl.BlockSpec(memory_space=pl.ANY),
            pl.BlockSpec((1, 1, 2 * D_FF), lambda e, *_: (e, 0, 0)),
            pl.BlockSpec(memory_space=pl.ANY),
            pl.BlockSpec((1, 1, D_MODEL), lambda e, *_: (e, 0, 0)),
        ],
        out_specs=pl.BlockSpec(memory_space=pl.ANY),
        scratch_shapes=[
            pltpu.VMEM((2, D_MODEL, 2 * D_FF), f32),
            pltpu.VMEM((2, D_FF, D_MODEL), f32),
            pltpu.VMEM((D_MODEL, 2 * D_FF), bf16),
            pltpu.VMEM((D_FF, D_MODEL), bf16),
            pltpu.VMEM((2, tg, X_WIDTH), bf16),
            pltpu.VMEM((2, tg, D_MODEL), bf16),
            pltpu.SMEM((2,), jnp.int32),
            pltpu.SemaphoreType.DMA((2,)),
            pltpu.SemaphoreType.DMA((2,)),
            pltpu.SemaphoreType.DMA((2,)),
        ],
    )
    return pl.pallas_call(
        functools.partial(_moe_kernel, n_group_tiles=n_rows // tg),
        grid_spec=grid_spec,
        out_shape=jax.ShapeDtypeStruct((n_rows, D_MODEL), bf16),
        compiler_params=pltpu.CompilerParams(
            dimension_semantics=("arbitrary",), vmem_limit_bytes=VMEM_LIMIT),
        name="moe_experts",
    )(plan["region_start"], plan["region_tiles"], plan["n_used"], xs, w_gu,
      b_gu.reshape(N_EXPERTS, 1, 2 * D_FF), w_down, b_down.reshape(N_EXPERTS, 1, D_MODEL))


def _combine_kernel(pdest_ref, npiece_ref, h1_ref, col_ref, g_ref, ys_ref, o_ref, ybuf, sems,
                    *, n_tiles):
    i = pl.program_id(0)
    slot = i % 2

    def piece_copy(t, s, j):
        src = pl.multiple_of(pdest_ref[t * PIECES_PER_TILE + j], SLOT_ROWS)
        return pltpu.make_async_copy(
            ys_ref.at[pl.ds(src, SLOT_ROWS)],
            ybuf.at[s, pl.ds(pl.multiple_of(j * SLOT_ROWS, SLOT_ROWS), SLOT_ROWS)], sems.at[s])

    def for_each_piece(t, s, action):
        for j in range(MIN_PIECES):
            action(piece_copy(t, s, j))

        def rest(j, c):
            action(piece_copy(t, s, j))
            return c
        lax.fori_loop(MIN_PIECES, npiece_ref[t], rest, 0)

    def fetch(t, s):
        for_each_piece(t, s, lambda c: c.start())

    @pl.when(i == 0)
    def _():
        ybuf[...] = jnp.zeros_like(ybuf)
        fetch(0, 0)

    @pl.when(i + 1 < n_tiles)
    def _():
        fetch(jnp.minimum(i + 1, n_tiles - 1), 1 - slot)

    for_each_piece(i, slot, lambda c: c.wait())

    col = col_ref[...]
    rows = [col[:, 3 * N_EXPERTS + k:3 * N_EXPERTS + k + 1].astype(jnp.int32) for k in range(TOP_K)]
    h = h1_ref[...]
    for c in range(TILE_SLOT_ROWS // PERM_CHUNK):
        r = lax.broadcasted_iota(jnp.int32, (col.shape[0], PERM_CHUNK), 1) + c * PERM_CHUNK
        g = jnp.zeros(r.shape, f32)
        for k in range(TOP_K):
            g = jnp.where(rows[k] == r, 1.0, g)
        h = h + jnp.dot(g.astype(bf16), ybuf[slot, c * PERM_CHUNK:(c + 1) * PERM_CHUNK, :],
                        preferred_element_type=f32)
    o_ref[...] = h * lax.rsqrt(jnp.mean(h * h, axis=-1, keepdims=True) + RMS_EPS) * g_ref[...]


def _combine(h1, ys, col, plan, norm_final_g):
    n = h1.shape[0]
    tm = ROW_TILE
    nt = n // tm
    grid_spec = pltpu.PrefetchScalarGridSpec(
        num_scalar_prefetch=2,
        grid=(nt,),
        in_specs=[
            pl.BlockSpec((tm, D_MODEL), lambda i, *_: (i, 0)),
            pl.BlockSpec((tm, LANES), lambda i, *_: (i, 0)),
            pl.BlockSpec((1, D_MODEL), lambda i, *_: (0, 0)),
            pl.BlockSpec(memory_space=pl.ANY),
        ],
        out_specs=pl.BlockSpec((tm, D_MODEL), lambda i, *_: (i, 0)),
        scratch_shapes=[
            pltpu.VMEM((2, TILE_SLOT_ROWS, D_MODEL), bf16),
            pltpu.SemaphoreType.DMA((2,)),
        ],
    )
    return pl.pallas_call(
        functools.partial(_combine_kernel, n_tiles=nt),
        grid_spec=grid_spec,
        out_shape=jax.ShapeDtypeStruct((n, D_MODEL), f32),
        compiler_params=pltpu.CompilerParams(
            dimension_semantics=("arbitrary",), vmem_limit_bytes=VMEM_LIMIT),
        name="combine",
    )(plan["piece_dest"], plan["n_pieces"], h1, col, norm_final_g.reshape(1, D_MODEL), ys)


def _routing_plan(slot_counts):
    tg = GROUP_TILE
    i32 = jnp.int32
    pc = slot_counts.astype(i32) * SLOT_ROWS
    local_end = jnp.cumsum(pc, axis=1)
    local_start = local_end - pc
    total = jnp.sum(pc, axis=0)
    region = ((total + tg - 1) // tg) * tg
    region_end = jnp.cumsum(region)
    region_start = region_end - region
    base = region_start[None, :] + jnp.cumsum(pc, axis=0) - pc
    n_pieces = local_end[:, -1] // SLOT_ROWS

    piece_row = jnp.arange(PIECES_PER_TILE, dtype=i32) * SLOT_ROWS
    owner = jnp.sum((local_end[:, None, :] <= piece_row[None, :, None]).astype(i32), axis=2)
    owner_hit = owner[:, :, None] == jnp.arange(N_EXPERTS, dtype=i32)[None, None, :]
    shift = jnp.sum(jnp.where(owner_hit, (base - local_start)[:, None, :], 0), axis=2)
    piece_dest = shift + piece_row[None, :]
    piece_dest = jnp.where(piece_row[None, :] < local_end[:, -1:], piece_dest, 0)

    pad_slots = (region - total) // SLOT_ROWS
    pad_end = jnp.cumsum(pad_slots)
    m = jnp.arange(N_EXPERTS * (tg // SLOT_ROWS), dtype=i32)
    pad_owner = jnp.minimum(jnp.sum((pad_end[None, :] <= m[:, None]).astype(i32), axis=1),
                            N_EXPERTS - 1)
    pad_hit = pad_owner[:, None] == jnp.arange(N_EXPERTS, dtype=i32)[None, :]
    pad_first = jnp.sum(jnp.where(pad_hit, (region_start + total - (pad_end - pad_slots) * SLOT_ROWS)
                                  [None, :], 0), axis=1)
    pad_dest = jnp.where(m < pad_end[-1], pad_first + m * SLOT_ROWS, 0)

    return {
        "piece_dest": piece_dest.reshape(-1), "n_pieces": n_pieces, "pad_dest": pad_dest,
        "n_pad": pad_end[-1:],
        "region_start": region_start, "region_tiles": region // tg,
        "n_used": (region_end[-1] // tg).reshape(1),
    }


def kernel(x, norm_mix_g, w_in, b_f, b_glu, w_dw, b_dw, ln_g, ln_b, w_out, norm_ffn_g, w_router,
           b_router, w_gu, b_gu, w_down, b_down, norm_final_g):
    batch, seq, d = x.shape
    assert d == D_MODEL and seq % (ATT_Q_TILES * ROW_TILE) == 0, (batch, seq, d)
    n = batch * seq
    x2d = x.reshape(n, d)

    qa, kta, v, conv = _inproj(x2d, norm_mix_g, w_in, b_f, b_glu, w_dw, b_dw, ln_g, ln_b,
                               batch=batch, seq=seq)
    att = _attention(qa, kta, v.reshape(batch, seq, D_ATT), batch=batch, seq=seq)

    h1, hna, dl, col, slot_counts = _outproj_router(
        att.reshape(n, D_ATT), conv, x2d, w_out, norm_ffn_g, w_router, b_router)

    n_token_tiles = n // ROW_TILE
    max_rows = n * TOP_K + N_EXPERTS * (n_token_tiles * (SLOT_ROWS - 1) + GROUP_TILE - 1)
    n_group_tiles = -(-max_rows // GROUP_TILE)
    plan = _routing_plan(slot_counts[:, :, 0])
    xs = _dispatch(hna, dl, plan, n_group_tiles * GROUP_TILE)
    ys = _moe(xs, plan, w_gu, b_gu, w_down, b_down)
    out = _combine(h1, ys, col, plan, norm_final_g)
    return out.reshape(batch, seq, d)
```

```python
import functools
import math

import numpy as np
import jax
import jax.numpy as jnp
from jax import lax
from jax.experimental import pallas as pl
from jax.experimental.pallas import tpu as pltpu

D_MODEL = 1024
ATT_HEADS = 8
HEAD_DIM = 64
D_ATT = ATT_HEADS * HEAD_DIM
D_CONV = D_MODEL - D_ATT
CONV_WIDTH = 31
N_EXPERTS = 32
TOP_K = 4
D_FF = 1024
SWIGLU_LIMIT = 7.0
SWIGLU_ALPHA = 1.702
RMS_EPS = 1e-6
LN_EPS = 1e-5
MASK_VALUE = -1e30
LOG2_E = 1.4426950408889634

LANES = 128
SUBLANES = 8
CONV_ROWS = 64
ROW_TILE = 512
ATT_SUB = 512
ATT_Q_TILES = 2
CONV_HALO = 32
GROUP_TILE = 256
SLOT_ROWS = 16
X_WIDTH = D_MODEL + LANES
TILE_SLOT_ROWS = ROW_TILE * TOP_K + N_EXPERTS * SLOT_ROWS
PIECES_PER_TILE = TILE_SLOT_ROWS // SLOT_ROWS
MIN_PIECES = ROW_TILE * TOP_K // SLOT_ROWS
PERM_CHUNK = 512
PERM_CHUNKS = tuple((s, PERM_CHUNK) for s in range(0, ROW_TILE * TOP_K, PERM_CHUNK)) + (
    (ROW_TILE * TOP_K, N_EXPERTS * SLOT_ROWS // 2),)
PERM_OPTIONAL = (ROW_TILE * TOP_K + N_EXPERTS * SLOT_ROWS // 2, N_EXPERTS * SLOT_ROWS // 2)
VMEM_LIMIT = 56 * 1024 * 1024

AUG_Q = HEAD_DIM
AUG_K = HEAD_DIM + 3
ONES_LANE = ATT_HEADS
PART_STRIDE = 16

f32 = jnp.float32
bf16 = jnp.bfloat16


def _split3(x):
    hi = x.astype(bf16)
    r1 = x - hi.astype(f32)
    mid = r1.astype(bf16)
    lo = (r1 - mid.astype(f32)).astype(bf16)
    return hi, mid, lo


def _selection_matrices():
    sel_q = np.zeros((LANES, ATT_HEADS * LANES), np.float32)
    sel_k = np.zeros((ATT_HEADS * LANES, LANES), np.float32)
    for h in range(ATT_HEADS):
        base = h * LANES
        for p in range(3):
            sel_q[PART_STRIDE * p + h, base + AUG_Q + p] = 1.0
            sel_q[ONES_LANE, base + AUG_K + p] = 1.0
            sel_k[base + AUG_K + p, PART_STRIDE * p + h] = -1.0
            sel_k[base + AUG_Q + p, ONES_LANE] = 1.0
    return jnp.asarray(sel_q, bf16), jnp.asarray(sel_k, bf16)


def _inproj_kernel(x_ref, g_ref, wq_ref, wkt_ref, wv_ref, wf_ref, wglu_ref, bf_ref, bglu_ref,
                   selq_ref, selk_ref, tri_ref, wdw_ref, bdw_ref, lng_ref, lnb_ref,
                   qa_ref, kta_ref, v_ref, conv_ref, carry_ref, ext_ref, *, tiles_per_seq):
    i = pl.program_id(0)
    tm = x_ref.shape[0]

    @pl.when(i % tiles_per_seq == 0)
    def _():
        carry_ref[...] = jnp.zeros_like(carry_ref)
        ext_ref[0, 0:CONV_HALO, :] = jnp.zeros((CONV_HALO, D_CONV), f32)

    x = x_ref[...]
    xn = x * lax.rsqrt(jnp.mean(x * x, axis=-1, keepdims=True) + RMS_EPS) * g_ref[...]
    xb = xn.astype(bf16)

    glu = jnp.dot(xb, wglu_ref[...], preferred_element_type=f32) + bglu_ref[...]
    ext_ref[0, CONV_HALO:, :] = glu[:, :D_CONV] * (1.0 / (1.0 + jnp.exp(-glu[:, D_CONV:])))
    span = tm + CONV_HALO - SUBLANES
    for s in range(1, SUBLANES):
        ext_ref[s, 0:span, :] = ext_ref[0, s:s + span, :]
    shift = CONV_HALO - (CONV_WIDTH - 1)
    for cb in range(tm // CONV_ROWS):
        r0 = cb * CONV_ROWS
        acc = jnp.zeros((CONV_ROWS, D_CONV), f32) + bdw_ref[...]
        for j in range(CONV_WIDTH):
            s = (shift + j) % SUBLANES
            a = shift + j - s
            acc = acc + wdw_ref[j:j + 1, :] * ext_ref[s, r0 + a:r0 + a + CONV_ROWS, :]
        mu = jnp.mean(acc, axis=-1, keepdims=True)
        d = acc - mu
        var = jnp.mean(d * d, axis=-1, keepdims=True)
        y = d * lax.rsqrt(var + LN_EPS) * lng_ref[...] + lnb_ref[...]
        conv_ref[r0:r0 + CONV_ROWS, :] = (y * (1.0 / (1.0 + jnp.exp(-y)))).astype(conv_ref.dtype)
    ext_ref[0, 0:CONV_HALO, :] = ext_ref[0, tm:tm + CONV_HALO, :]

    f = jnp.dot(xb, wf_ref[...], preferred_element_type=f32) + bf_ref[...]
    log_f = jnp.minimum(f, 0.0) - jnp.log1p(jnp.exp(-jnp.abs(f)))
    lane = lax.broadcasted_iota(jnp.int32, log_f.shape, 1)
    log_f = jnp.where(((lane & (PART_STRIDE - 1)) < ATT_HEADS) & (lane < 3 * PART_STRIDE), log_f, 0.0)
    tri = tri_ref[...]
    hi, mid, lo = _split3(log_f)
    c = (jnp.dot(tri, hi, preferred_element_type=f32)
         + jnp.dot(tri, mid, preferred_element_type=f32)
         + jnp.dot(tri, lo, preferred_element_type=f32)) + carry_ref[...]
    carry_ref[...] = c[tm - 1:tm, :]
    c = c * LOG2_E

    c_hi = c.astype(bf16).astype(f32)
    c_mid = (c - c_hi).astype(bf16).astype(f32)
    c_lo = c - c_hi - c_mid
    parts = jnp.where(lane < PART_STRIDE, c_hi, jnp.where(lane < 2 * PART_STRIDE, c_mid, c_lo))
    parts = jnp.where(lane == ONES_LANE, 1.0, parts)

    qa = jnp.dot(xb, wq_ref[...], preferred_element_type=f32) * (LOG2_E / math.sqrt(HEAD_DIM))
    qa = (qa + jnp.dot(parts.astype(bf16), selq_ref[...], preferred_element_type=f32)).astype(bf16)
    for h in range(ATT_HEADS):
        qa_ref[0, h] = qa[:, h * LANES:(h + 1) * LANES]

    kta = lax.dot_general(wkt_ref[...], xb, (((1,), (1,)), ((), ())), preferred_element_type=f32)
    kta = (kta + jnp.dot(selk_ref[...], parts.T.astype(bf16), preferred_element_type=f32)).astype(bf16)
    for h in range(ATT_HEADS):
        kta_ref[0, h, 0] = kta[h * LANES:(h + 1) * LANES, :]

    v_ref[...] = jnp.dot(xb, wv_ref[...], preferred_element_type=f32).astype(bf16)


def _inproj(x2d, norm_mix_g, w_in, b_f, b_glu, w_dw, b_dw, ln_g, ln_b, *, batch, seq):
    n = batch * seq
    tm = ROW_TILE
    nt = seq // tm
    wq = w_in[:, :D_ATT]
    wk = w_in[:, D_ATT:2 * D_ATT]
    wv = w_in[:, 2 * D_ATT:3 * D_ATT]
    wf = w_in[:, 3 * D_ATT:3 * D_ATT + ATT_HEADS]
    wglu = w_in[:, 3 * D_ATT + ATT_HEADS:]
    pad = LANES - HEAD_DIM
    wq_pad = jnp.pad(wq.reshape(D_MODEL, ATT_HEADS, HEAD_DIM), ((0, 0), (0, 0), (0, pad)))
    wq_pad = wq_pad.reshape(D_MODEL, ATT_HEADS * LANES).astype(bf16)
    wkt_pad = jnp.pad(wk.T.reshape(ATT_HEADS, HEAD_DIM, D_MODEL), ((0, 0), (0, pad), (0, 0)))
    wkt_pad = wkt_pad.reshape(ATT_HEADS * LANES, D_MODEL).astype(bf16)
    spread = lambda a: jnp.pad(jnp.concatenate(
        [jnp.pad(a, ((0, 0), (0, PART_STRIDE - ATT_HEADS)))] * 3, axis=1),
        ((0, 0), (0, LANES - 3 * PART_STRIDE)))
    wf_pad = spread(wf).astype(bf16)
    bf_pad = spread(b_f.reshape(1, ATT_HEADS))
    sel_q, sel_k = _selection_matrices()
    tri = jnp.asarray(np.tril(np.ones((tm, tm), np.float32)), bf16)

    const = lambda shape: pl.BlockSpec(shape, lambda i: (0,) * len(shape))
    return pl.pallas_call(
        functools.partial(_inproj_kernel, tiles_per_seq=nt),
        grid=(n // tm,),
        in_specs=[
            pl.BlockSpec((tm, D_MODEL), lambda i: (i, 0)),
            const((1, D_MODEL)),
            const((D_MODEL, ATT_HEADS * LANES)),
            const((ATT_HEADS * LANES, D_MODEL)),
            const((D_MODEL, D_ATT)),
            const((D_MODEL, LANES)),
            const((D_MODEL, 2 * D_CONV)),
            const((1, LANES)),
            const((1, 2 * D_CONV)),
            const((LANES, ATT_HEADS * LANES)),
            const((ATT_HEADS * LANES, LANES)),
            const((tm, tm)),
            const((CONV_HALO, D_CONV)), const((1, D_CONV)), const((1, D_CONV)), const((1, D_CONV)),
        ],
        out_specs=[
            pl.BlockSpec((1, ATT_HEADS, tm, LANES), lambda i: (i // nt, 0, i % nt, 0)),
            pl.BlockSpec((1, ATT_HEADS, 1, LANES, tm), lambda i: (i // nt, 0, i % nt, 0, 0)),
            pl.BlockSpec((tm, D_ATT), lambda i: (i, 0)),
            pl.BlockSpec((tm, D_CONV), lambda i: (i, 0)),
        ],
        out_shape=[
            jax.ShapeDtypeStruct((batch, ATT_HEADS, seq, LANES), bf16),
            jax.ShapeDtypeStruct((batch, ATT_HEADS, nt, LANES, tm), bf16),
            jax.ShapeDtypeStruct((n, D_ATT), bf16),
            jax.ShapeDtypeStruct((n, D_CONV), bf16),
        ],
        scratch_shapes=[pltpu.VMEM((1, LANES), f32),
                        pltpu.VMEM((SUBLANES, tm + CONV_HALO, D_CONV), f32)],
        compiler_params=pltpu.CompilerParams(
            dimension_semantics=("arbitrary",), vmem_limit_bytes=VMEM_LIMIT),
        name="inproj",
    )(x2d, norm_mix_g.reshape(1, D_MODEL), wq_pad, wkt_pad, wv.astype(bf16), wf_pad,
      wglu.astype(bf16), bf_pad, b_glu.reshape(1, 2 * D_CONV), sel_q, sel_k, tri,
      jnp.pad(w_dw, ((0, CONV_HALO - CONV_WIDTH), (0, 0))), b_dw.reshape(1, D_CONV),
      ln_g.reshape(1, D_CONV), ln_b.reshape(1, D_CONV))


def _attn_kernel(qa_ref, kta_ref, v_ref, o_ref, acc_ref, m_ref, *, tile):
    i = pl.program_id(2)
    q_rows = ATT_Q_TILES * tile
    first_half = lax.broadcasted_iota(jnp.int32, (tile, LANES), 1) < HEAD_DIM
    acc_ref[...] = jnp.zeros_like(acc_ref)
    m_ref[...] = jnp.full_like(m_ref, MASK_VALUE)
    sub = ATT_SUB
    blocks_per_tile = tile // sub

    def process(kv_tiles):
        v_augs = []
        for j, _ in kv_tiles:
            vp = v_ref[0, pl.ds(pl.multiple_of(j * tile, tile), tile), :]
            one = jnp.ones_like(vp)
            v_augs.append((jnp.where(first_half, vp, one), jnp.where(first_half, one, vp)))
        chains = [(t, h, rb) for t in range(len(kv_tiles)) for h in range(2)
                  for rb in range(q_rows // sub) if kv_tiles[t][1][rb // blocks_per_tile] is not None]

        def visible_width(t, rb):
            if kv_tiles[t][1][rb // blocks_per_tile] == "diag":
                return (rb % blocks_per_tile + 1) * sub
            return tile

        def scores(t, h, rb):
            return jnp.dot(qa_ref[0, h, rb * sub:(rb + 1) * sub, :],
                           kta_ref[0, h, kv_tiles[t][0]][:, :visible_width(t, rb)],
                           preferred_element_type=f32)

        s_next = scores(*chains[0])
        for ci, (t, h, rb) in enumerate(chains):
            s = s_next
            if ci + 1 < len(chains):
                s_next = scores(*chains[ci + 1])
            rows = slice(rb * sub, (rb + 1) * sub)
            width = s.shape[1]
            if kv_tiles[t][1][rb // blocks_per_tile] == "diag":
                r = lax.broadcasted_iota(jnp.int32, s.shape, 0) + (rb % blocks_per_tile) * sub
                c = lax.broadcasted_iota(jnp.int32, s.shape, 1)
                s = jnp.where(c <= r, s, MASK_VALUE)
            m_old = m_ref[h, rows, :]
            m_new = jnp.maximum(m_old, jnp.max(s, axis=-1, keepdims=True))
            alpha = jnp.exp2(m_old - m_new)
            p = jnp.exp2(s - jnp.concatenate([m_new] * (width // LANES), axis=1)).astype(bf16)
            acc_ref[h, rows, :] = alpha * acc_ref[h, rows, :] + jnp.dot(
                p, v_augs[t][h][:width, :], preferred_element_type=f32)
            m_ref[h, rows, :] = m_new

    everything = ("full",) * ATT_Q_TILES

    def body(jj, carry):
        process([(ATT_Q_TILES * jj + t, everything) for t in range(ATT_Q_TILES)])
        return carry

    lax.fori_loop(0, i, body, 0)
    process([(ATT_Q_TILES * i + t,
              tuple("full" if t < qt else ("diag" if t == qt else None) for qt in range(ATT_Q_TILES)))
             for t in range(ATT_Q_TILES)])

    for qt in range(ATT_Q_TILES):
        a0 = acc_ref[0, qt * tile:(qt + 1) * tile, :]
        a1 = acc_ref[1, qt * tile:(qt + 1) * tile, :]
        l0 = a0[:, HEAD_DIM:HEAD_DIM + 1]
        l1 = a1[:, 0:1]
        o_ref[0, qt * tile:(qt + 1) * tile, :] = jnp.where(
            first_half, a0 / l0, a1 / l1).astype(o_ref.dtype)


def _attention(qa, kta, v3, *, batch, seq):
    tile = ROW_TILE
    nt = seq // tile
    q_rows = ATT_Q_TILES * tile
    return pl.pallas_call(
        functools.partial(_attn_kernel, tile=tile),
        grid=(batch, ATT_HEADS // 2, seq // q_rows),
        in_specs=[
            pl.BlockSpec((1, 2, q_rows, LANES), lambda b, hp, i: (b, hp, i, 0)),
            pl.BlockSpec((1, 2, nt, LANES, tile), lambda b, hp, i: (b, hp, 0, 0, 0)),
            pl.BlockSpec((1, seq, LANES), lambda b, hp, i: (b, 0, hp)),
        ],
        out_specs=pl.BlockSpec((1, q_rows, LANES), lambda b, hp, i: (b, i, hp)),
        out_shape=jax.ShapeDtypeStruct((batch, seq, D_ATT), bf16),
        scratch_shapes=[pltpu.VMEM((2, q_rows, LANES), f32), pltpu.VMEM((2, q_rows, LANES), f32)],
        compiler_params=pltpu.CompilerParams(
            dimension_semantics=("arbitrary", "arbitrary", "arbitrary"),
            vmem_limit_bytes=VMEM_LIMIT),
        name="attention",
    )(qa, kta, v3)


def _outproj_router_kernel(att_ref, conv_ref, x_ref, wo_ref, g_ref, wr_hi_ref, wr_lo_ref, br_ref,
                           triu_ref, tril_ref, h1_ref, hna_ref, dl_ref, col_ref, cnt_ref):
    h1 = (x_ref[...]
          + jnp.dot(att_ref[...], wo_ref[:D_ATT, :], preferred_element_type=f32)
          + jnp.dot(conv_ref[...], wo_ref[D_ATT:, :], preferred_element_type=f32))
    h1_ref[...] = h1
    hn = h1 * lax.rsqrt(jnp.mean(h1 * h1, axis=-1, keepdims=True) + RMS_EPS) * g_ref[...]

    hn_hi = hn.astype(bf16)
    hn_lo = (hn - hn_hi.astype(f32)).astype(bf16)
    nt = (((1,), (1,)), ((), ()))
    logits = (lax.dot_general(wr_hi_ref[...], hn_hi, nt, preferred_element_type=f32)
              + lax.dot_general(wr_hi_ref[...], hn_lo, nt, preferred_element_type=f32)
              + lax.dot_general(wr_lo_ref[...], hn_hi, nt, preferred_element_type=f32)
              + br_ref[...])

    erow = lax.broadcasted_iota(jnp.int32, logits.shape, 0)
    work = logits
    vals, idxs = [], []
    for _ in range(TOP_K):
        mk = jnp.max(work, axis=0, keepdims=True)
        ik = jnp.min(jnp.where(work == mk, erow, N_EXPERTS), axis=0, keepdims=True)
        work = jnp.where(erow == ik, -jnp.inf, work)
        vals.append(mk)
        idxs.append(ik)
    exps = [jnp.exp(v - vals[0]) for v in vals]
    denom = exps[0] + exps[1] + exps[2] + exps[3]

    onehot = jnp.zeros(logits.shape, f32)
    gates = jnp.zeros(logits.shape, f32)
    for k in range(TOP_K):
        hit = erow == idxs[k]
        onehot = onehot + jnp.where(hit, 1.0, 0.0)
        gates = gates + jnp.where(hit, exps[k] / denom, 0.0)

    before = jnp.dot(onehot.astype(bf16), triu_ref[...], preferred_element_type=f32)
    count = jnp.sum(onehot, axis=1, keepdims=True)
    slots = jnp.floor((count + (SLOT_ROWS - 1)) * (1.0 / SLOT_ROWS))
    slots_b = jnp.broadcast_to(slots, (N_EXPERTS, LANES))
    slot_start = jnp.dot(tril_ref[...], slots_b.astype(bf16), preferred_element_type=f32)
    cnt_ref[0] = slots_b
    pos = before + slot_start[:, 0:1] * SLOT_ROWS
    dls = []
    for k in range(TOP_K):
        dk = jnp.sum(jnp.where(erow == idxs[k], pos, 0.0), axis=0, keepdims=True)
        dl_ref[k:k + 1, :] = dk.astype(jnp.int32)
        dls.append(dk)

    g_hi, g_mid, g_lo = _split3(gates)
    tm = logits.shape[1]
    r8 = lax.broadcasted_iota(jnp.int32, (8, tm), 0)
    dl8 = jnp.zeros((8, tm), f32)
    for k in range(TOP_K):
        dl8 = jnp.where(r8 == k, dls[k], dl8)
    stack = jnp.concatenate(
        [g_hi.astype(f32), g_mid.astype(f32), g_lo.astype(f32), dl8,
         jnp.zeros((LANES - 3 * N_EXPERTS - 8, tm), f32)], axis=0)
    col = stack.T
    col_ref[...] = col
    lane = lax.broadcasted_iota(jnp.int32, col.shape, 1)
    hna_ref[:, :D_MODEL] = hn_hi
    hna_ref[:, D_MODEL:] = jnp.where(lane < 3 * N_EXPERTS, col, 0.0).astype(bf16)


def _outproj_router(att2d, conv2d, x2d, w_out, norm_ffn_g, w_router, b_router):
    n = x2d.shape[0]
    tm = ROW_TILE
    wr_t = w_router.T
    wr_hi = wr_t.astype(bf16)
    wr_lo = (wr_t - wr_hi.astype(f32)).astype(bf16)
    triu = jnp.asarray(np.triu(np.ones((tm, tm), np.float32), k=1), bf16)
    tril = jnp.asarray(np.tril(np.ones((N_EXPERTS, N_EXPERTS), np.float32), k=-1), bf16)
    const = lambda shape: pl.BlockSpec(shape, lambda i: (0,) * len(shape))
    rows = lambda width: pl.BlockSpec((tm, width), lambda i: (i, 0))
    return pl.pallas_call(
        _outproj_router_kernel,
        grid=(n // tm,),
        in_specs=[
            rows(D_ATT), rows(D_CONV), rows(D_MODEL),
            const((D_MODEL, D_MODEL)), const((1, D_MODEL)),
            const((N_EXPERTS, D_MODEL)), const((N_EXPERTS, D_MODEL)), const((N_EXPERTS, 1)),
            const((tm, tm)), const((N_EXPERTS, N_EXPERTS)),
        ],
        out_specs=[
            rows(D_MODEL), rows(X_WIDTH),
            pl.BlockSpec((TOP_K, tm), lambda i: (0, i)),
            rows(LANES),
            pl.BlockSpec((1, N_EXPERTS, LANES), lambda i: (i, 0, 0)),
        ],
        out_shape=[
            jax.ShapeDtypeStruct((n, D_MODEL), f32),
            jax.ShapeDtypeStruct((n, X_WIDTH), bf16),
            jax.ShapeDtypeStruct((TOP_K, n), jnp.int32),
            jax.ShapeDtypeStruct((n, LANES), f32),
            jax.ShapeDtypeStruct((n // tm, N_EXPERTS, LANES), f32),
        ],
        compiler_params=pltpu.CompilerParams(
            dimension_semantics=("arbitrary",), vmem_limit_bytes=VMEM_LIMIT),
        name="outproj_router",
    )(att2d, conv2d, x2d, w_out.astype(bf16), norm_ffn_g.reshape(1, D_MODEL), wr_hi, wr_lo,
      b_router.reshape(N_EXPERTS, 1), triu, tril)


def _dispatch_kernel(pdest_ref, npiece_ref, padd_ref, npad_ref, nused_ref, dl_ref, hna_ref, xs_ref,
                     xbuf, zbuf, sems, zsem, *, n_tiles, n_group_tiles):
    i = pl.program_id(0)
    slot = i % 2

    def piece_copy(t, s, j):
        dst = pl.multiple_of(pdest_ref[t * PIECES_PER_TILE + j], SLOT_ROWS)
        return pltpu.make_async_copy(
            xbuf.at[s, pl.ds(pl.multiple_of(j * SLOT_ROWS, SLOT_ROWS), SLOT_ROWS)],
            xs_ref.at[pl.ds(dst, SLOT_ROWS)], sems.at[s])

    def for_each_piece(t, s, action):
        for j in range(MIN_PIECES):
            action(piece_copy(t, s, j))

        def rest(j, c):
            action(piece_copy(t, s, j))
            return c
        lax.fori_loop(MIN_PIECES, npiece_ref[t], rest, 0)

    def pad_copy(m):
        dst = pl.multiple_of(padd_ref[m], SLOT_ROWS)
        return pltpu.make_async_copy(
            zbuf.at[pl.ds(0, SLOT_ROWS)], xs_ref.at[pl.ds(dst, SLOT_ROWS)], zsem)

    def tail_copy(t):
        dst = pl.multiple_of(t * GROUP_TILE, GROUP_TILE)
        return pltpu.make_async_copy(zbuf, xs_ref.at[pl.ds(dst, GROUP_TILE)], zsem)

    @pl.when(i >= 2)
    def _():
        for_each_piece(jnp.maximum(i - 2, 0), slot, lambda c: c.wait())

    @pl.when(i < n_tiles)
    def _():
        tile_idx = jnp.minimum(i, n_tiles - 1)

        def permute_rows(start, size):
            dl = dl_ref[...]
            r = lax.broadcasted_iota(jnp.int32, (size, dl.shape[1]), 0) + start
            p = jnp.zeros(r.shape, f32)
            for k in range(TOP_K):
                p = jnp.where(dl[k:k + 1, :] == r, 1.0, p)
            xbuf[slot, start:start + size, :] = jnp.dot(
                p.astype(bf16), hna_ref[...], preferred_element_type=f32).astype(bf16)

        for start, size in PERM_CHUNKS:
            permute_rows(start, size)

        @pl.when(npiece_ref[tile_idx] * SLOT_ROWS > PERM_OPTIONAL[0])
        def _():
            permute_rows(*PERM_OPTIONAL)

        for_each_piece(tile_idx, slot, lambda c: c.start())

    @pl.when(i == n_tiles)
    def _():
        zbuf[...] = jnp.zeros_like(zbuf)

        def start(m, c):
            pad_copy(m).start()
            return c
        lax.fori_loop(0, npad_ref[0], start, 0)

        def start_tail(t, c):
            tail_copy(t).start()
            return c
        lax.fori_loop(nused_ref[0], n_group_tiles, start_tail, 0)

    @pl.when(i == n_tiles + 1)
    def _():
        def wait(m, c):
            pad_copy(m).wait()
            return c
        lax.fori_loop(0, npad_ref[0], wait, 0)

        def wait_tail(t, c):
            tail_copy(t).wait()
            return c
        lax.fori_loop(nused_ref[0], n_group_tiles, wait_tail, 0)


def _dispatch(hna, dl, plan, n_rows):
    n = hna.shape[0]
    tm = ROW_TILE
    nt = n // tm
    last = nt - 1
    grid_spec = pltpu.PrefetchScalarGridSpec(
        num_scalar_prefetch=5,
        grid=(nt + 2,),
        in_specs=[
            pl.BlockSpec((TOP_K, tm), lambda i, *_: (0, jnp.minimum(i, last))),
            pl.BlockSpec((tm, X_WIDTH), lambda i, *_: (jnp.minimum(i, last), 0)),
        ],
        out_specs=pl.BlockSpec(memory_space=pl.ANY),
        scratch_shapes=[
            pltpu.VMEM((2, TILE_SLOT_ROWS, X_WIDTH), bf16),
            pltpu.VMEM((GROUP_TILE, X_WIDTH), bf16),
            pltpu.SemaphoreType.DMA((2,)),
            pltpu.SemaphoreType.DMA,
        ],
    )
    return pl.pallas_call(
        functools.partial(_dispatch_kernel, n_tiles=nt, n_group_tiles=n_rows // GROUP_TILE),
        grid_spec=grid_spec,
        out_shape=jax.ShapeDtypeStruct((n_rows, X_WIDTH), bf16),
        compiler_params=pltpu.CompilerParams(
            dimension_semantics=("arbitrary",), vmem_limit_bytes=VMEM_LIMIT),
        name="dispatch",
    )(plan["piece_dest"], plan["n_pieces"], plan["pad_dest"], plan["n_pad"], plan["n_used"], dl, hna)


def _moe_kernel(rstart_ref, rtiles_ref, nused_ref, xs_ref, wgu_hbm, bgu_ref, wd_hbm, bd_ref, ys_ref,
                wgu_f32, wd_f32, wgu_bf, wd_bf, xbuf, ybuf, pend, wsem, xsem, ysem, *, n_group_tiles):
    e = pl.program_id(0)
    n_t = rtiles_ref[e]
    row0 = rstart_ref[e]
    ws = e % 2

    def w_copies(ex, slot):
        return (pltpu.make_async_copy(wgu_hbm.at[ex], wgu_f32.at[slot], wsem.at[slot]),
                pltpu.make_async_copy(wd_hbm.at[ex], wd_f32.at[slot], wsem.at[slot]))

    def x_copy(t, s):
        src = pl.multiple_of(row0 + t * GROUP_TILE, GROUP_TILE)
        return pltpu.make_async_copy(xs_ref.at[pl.ds(src, GROUP_TILE)], xbuf.at[s], xsem.at[s])

    def y_copy(t, s):
        dst = pl.multiple_of(row0 + t * GROUP_TILE, GROUP_TILE)
        return pltpu.make_async_copy(ybuf.at[s], ys_ref.at[pl.ds(dst, GROUP_TILE)], ysem.at[s])

    def tail_copy(t):
        dst = pl.multiple_of(t * GROUP_TILE, GROUP_TILE)
        return pltpu.make_async_copy(ybuf.at[0], ys_ref.at[pl.ds(dst, GROUP_TILE)], ysem.at[0])

    @pl.when(e == 0)
    def _():
        pend[0] = 0
        pend[1] = 0
        for c in w_copies(0, 0):
            c.start()

    @pl.when(n_t > 0)
    def _():
        x_copy(0, 0).start()

    @pl.when(n_t > 1)
    def _():
        x_copy(1, 1).start()

    @pl.when(e + 1 < N_EXPERTS)
    def _():
        for c in w_copies(jnp.minimum(e + 1, N_EXPERTS - 1), 1 - ws):
            c.start()

    for c in w_copies(e, ws):
        c.wait()

    @pl.when(n_t > 0)
    def _():
        wgu_bf[...] = wgu_f32[ws].astype(bf16)
        wd_bf[...] = wd_f32[ws].astype(bf16)

        def body(t, carry):
            s = t % 2
            x_copy(t, s).wait()

            @pl.when((t >= 2) | (pend[s] == 1))
            def _():
                y_copy(t, s).wait()
                pend[s] = 0

            h = jnp.dot(xbuf[s, :, :D_MODEL], wgu_bf[...], preferred_element_type=f32) + bgu_ref[0]
            hg = jnp.minimum(h[:, :D_FF], SWIGLU_LIMIT)
            hu = jnp.clip(h[:, D_FF:], -SWIGLU_LIMIT, SWIGLU_LIMIT)
            act = (hu + 1.0) * (hg * (1.0 / (1.0 + jnp.exp(-SWIGLU_ALPHA * hg))))
            y = jnp.dot(act.astype(bf16), wd_bf[...], preferred_element_type=f32) + bd_ref[0]
            aug = xbuf[s, :, D_MODEL:].astype(f32)
            lane = lax.broadcasted_iota(jnp.int32, aug.shape, 1)
            gate = jnp.sum(jnp.where((lane & (N_EXPERTS - 1)) == e, aug, 0.0), axis=1, keepdims=True)
            ybuf[s] = (gate * y).astype(ybuf.dtype)
            y_copy(t, s).start()

            @pl.when(t + 2 < n_t)
            def _():
                x_copy(t + 2, s).start()

            return carry

        lax.fori_loop(0, n_t, body, 0)
        pend[(n_t - 1) % 2] = 1

        @pl.when(n_t >= 2)
        def _():
            pend[n_t % 2] = 1

    @pl.when(e == N_EXPERTS - 1)
    def _():
        for s in range(2):
            @pl.when(pend[s] == 1)
            def _():
                y_copy(0, s).wait()
                pend[s] = 0

        ybuf[0] = jnp.zeros(ybuf.shape[1:], ybuf.dtype)

        def start_tail(t, carry):
            tail_copy(t).start()
            return carry

        def wait_tail(t, carry):
            tail_copy(t).wait()
            return carry

        lax.fori_loop(nused_ref[0], n_group_tiles, start_tail, 0)
        lax.fori_loop(nused_ref[0], n_group_tiles, wait_tail, 0)


def _moe(xs, plan, w_gu, b_gu, w_down, b_down):
    n_rows = xs.shape[0]
    tg = GROUP_TILE
    grid_spec = pltpu.PrefetchScalarGridSpec(
        num_scalar_prefetch=3,
        grid=(N_EXPERTS,),
        in_specs=[
            pl.BlockSpec(memory_space=pl.ANY),
            pl.BlockSpec(memory_space=pl.ANY),
            pl.BlockSpec((1, 1, 2 * D_FF), lambda e, *_: (e, 0, 0)),
            pl.BlockSpec(memory_space=pl.ANY),
            pl.BlockSpec((1, 1, D_MODEL), lambda e, *_: (e, 0, 0)),
        ],
        out_specs=pl.BlockSpec(memory_space=pl.ANY),
        scratch_shapes=[
            pltpu.VMEM((2, D_MODEL, 2 * D_FF), f32),
            pltpu.VMEM((2, D_FF, D_MODEL), f32),
            pltpu.VMEM((D_MODEL, 2 * D_FF), bf16),
            pltpu.VMEM((D_FF, D_MODEL), bf16),
            pltpu.VMEM((2, tg, X_WIDTH), bf16),
            pltpu.VMEM((2, tg, D_MODEL), bf16),
            pltpu.SMEM((2,), jnp.int32),
            pltpu.SemaphoreType.DMA((2,)),
            pltpu.SemaphoreType.DMA((2,)),
            pltpu.SemaphoreType.DMA((2,)),
        ],
    )
    return pl.pallas_call(
        functools.partial(_moe_kernel, n_group_tiles=n_rows // tg),
        grid_spec=grid_spec,
        out_shape=jax.ShapeDtypeStruct((n_rows, D_MODEL), bf16),
        compiler_params=pltpu.CompilerParams(
            dimension_semantics=("arbitrary",), vmem_limit_bytes=VMEM_LIMIT),
        name="moe_experts",
    )(plan["region_start"], plan["region_tiles"], plan["n_used"], xs, w_gu,
      b_gu.reshape(N_EXPERTS, 1, 2 * D_FF), w_down, b_down.reshape(N_EXPERTS, 1, D_MODEL))


def _combine_kernel(pdest_ref, npiece_ref, h1_ref, col_ref, g_ref, ys_ref, o_ref, ybuf, sems,
                    *, n_tiles):
    i = pl.program_id(0)
    slot = i % 2

    def piece_copy(t, s, j):
        src = pl.multiple_of(pdest_ref[t * PIECES_PER_TILE + j], SLOT_ROWS)
        return pltpu.make_async_copy(
            ys_ref.at[pl.ds(src, SLOT_ROWS)],
            ybuf.at[s, pl.ds(pl.multiple_of(j * SLOT_ROWS, SLOT_ROWS), SLOT_ROWS)], sems.at[s])

    def for_each_piece(t, s, action):
        for j in range(MIN_PIECES):
            action(piece_copy(t, s, j))

        def rest(j, c):
            action(piece_copy(t, s, j))
            return c
        lax.fori_loop(MIN_PIECES, npiece_ref[t], rest, 0)

    def fetch(t, s):
        for_each_piece(t, s, lambda c: c.start())

    @pl.when(i == 0)
    def _():
        ybuf[...] = jnp.zeros_like(ybuf)
        fetch(0, 0)

    @pl.when(i + 1 < n_tiles)
    def _():
        fetch(jnp.minimum(i + 1, n_tiles - 1), 1 - slot)

    for_each_piece(i, slot, lambda c: c.wait())

    col = col_ref[...]
    rows = [col[:, 3 * N_EXPERTS + k:3 * N_EXPERTS + k + 1].astype(jnp.int32) for k in range(TOP_K)]

    def gathered(start, size):
        r = lax.broadcasted_iota(jnp.int32, (col.shape[0], size), 1) + start
        g = jnp.zeros(r.shape, f32)
        for k in range(TOP_K):
            g = jnp.where(rows[k] == r, 1.0, g)
        return jnp.dot(g.astype(bf16), ybuf[slot, start:start + size, :], preferred_element_type=f32)

    h = h1_ref[...]
    for start, size in PERM_CHUNKS:
        h = h + gathered(start, size)
    o_ref[...] = h

    @pl.when(npiece_ref[i] * SLOT_ROWS > PERM_OPTIONAL[0])
    def _():
        o_ref[...] += gathered(*PERM_OPTIONAL)

    h = o_ref[...]
    o_ref[...] = h * lax.rsqrt(jnp.mean(h * h, axis=-1, keepdims=True) + RMS_EPS) * g_ref[...]


def _combine(h1, ys, col, plan, norm_final_g):
    n = h1.shape[0]
    tm = ROW_TILE
    nt = n // tm
    grid_spec = pltpu.PrefetchScalarGridSpec(
        num_scalar_prefetch=2,
        grid=(nt,),
        in_specs=[
            pl.BlockSpec((tm, D_MODEL), lambda i, *_: (i, 0)),
            pl.BlockSpec((tm, LANES), lambda i, *_: (i, 0)),
            pl.BlockSpec((1, D_MODEL), lambda i, *_: (0, 0)),
            pl.BlockSpec(memory_space=pl.ANY),
        ],
        out_specs=pl.BlockSpec((tm, D_MODEL), lambda i, *_: (i, 0)),
        scratch_shapes=[
            pltpu.VMEM((2, TILE_SLOT_ROWS, D_MODEL), bf16),
            pltpu.SemaphoreType.DMA((2,)),
        ],
    )
    return pl.pallas_call(
        functools.partial(_combine_kernel, n_tiles=nt),
        grid_spec=grid_spec,
        out_shape=jax.ShapeDtypeStruct((n, D_MODEL), f32),
        compiler_params=pltpu.CompilerParams(
            dimension_semantics=("arbitrary",), vmem_limit_bytes=VMEM_LIMIT),
        name="combine",
    )(plan["piece_dest"], plan["n_pieces"], h1, col, norm_final_g.reshape(1, D_MODEL), ys)


def _routing_plan(slot_counts):
    tg = GROUP_TILE
    i32 = jnp.int32
    pc = slot_counts.astype(i32) * SLOT_ROWS
    local_end = jnp.cumsum(pc, axis=1)
    local_start = local_end - pc
    total = jnp.sum(pc, axis=0)
    region = ((total + tg - 1) // tg) * tg
    region_end = jnp.cumsum(region)
    region_start = region_end - region
    base = region_start[None, :] + jnp.cumsum(pc, axis=0) - pc
    n_pieces = local_end[:, -1] // SLOT_ROWS

    piece_row = jnp.arange(PIECES_PER_TILE, dtype=i32) * SLOT_ROWS
    owner = jnp.sum((local_end[:, None, :] <= piece_row[None, :, None]).astype(i32), axis=2)
    owner_hit = owner[:, :, None] == jnp.arange(N_EXPERTS, dtype=i32)[None, None, :]
    shift = jnp.sum(jnp.where(owner_hit, (base - local_start)[:, None, :], 0), axis=2)
    piece_dest = shift + piece_row[None, :]
    piece_dest = jnp.where(piece_row[None, :] < local_end[:, -1:], piece_dest, 0)

    pad_slots = (region - total) // SLOT_ROWS
    pad_end = jnp.cumsum(pad_slots)
    m = jnp.arange(N_EXPERTS * (tg // SLOT_ROWS), dtype=i32)
    pad_owner = jnp.minimum(jnp.sum((pad_end[None, :] <= m[:, None]).astype(i32), axis=1),
                            N_EXPERTS - 1)
    pad_hit = pad_owner[:, None] == jnp.arange(N_EXPERTS, dtype=i32)[None, :]
    pad_first = jnp.sum(jnp.where(pad_hit, (region_start + total - (pad_end - pad_slots) * SLOT_ROWS)
                                  [None, :], 0), axis=1)
    pad_dest = jnp.where(m < pad_end[-1], pad_first + m * SLOT_ROWS, 0)

    return {
        "piece_dest": piece_dest.reshape(-1), "n_pieces": n_pieces, "pad_dest": pad_dest,
        "n_pad": pad_end[-1:],
        "region_start": region_start, "region_tiles": region // tg,
        "n_used": (region_end[-1] // tg).reshape(1),
    }


def kernel(x, norm_mix_g, w_in, b_f, b_glu, w_dw, b_dw, ln_g, ln_b, w_out, norm_ffn_g, w_router,
           b_router, w_gu, b_gu, w_down, b_down, norm_final_g):
    batch, seq, d = x.shape
    assert d == D_MODEL and seq % (ATT_Q_TILES * ROW_TILE) == 0, (batch, seq, d)
    n = batch * seq
    x2d = x.reshape(n, d)

    qa, kta, v, conv = _inproj(x2d, norm_mix_g, w_in, b_f, b_glu, w_dw, b_dw, ln_g, ln_b,
                               batch=batch, seq=seq)
    att = _attention(qa, kta, v.reshape(batch, seq, D_ATT), batch=batch, seq=seq)

    h1, hna, dl, col, slot_counts = _outproj_router(
        att.reshape(n, D_ATT), conv, x2d, w_out, norm_ffn_g, w_router, b_router)

    n_token_tiles = n // ROW_TILE
    max_rows = n * TOP_K + N_EXPERTS * (n_token_tiles * (SLOT_ROWS - 1) + GROUP_TILE - 1)
    n_group_tiles = -(-max_rows // GROUP_TILE)
    plan = _routing_plan(slot_counts[:, :, 0])
    xs = _dispatch(hna, dl, plan, n_group_tiles * GROUP_TILE)
    ys = _moe(xs, plan, w_gu, b_gu, w_down, b_down)
    out = _combine(h1, ys, col, plan, norm_final_g)
    return out.reshape(batch, seq, d)
```

```python
import functools
import math

import numpy as np
import jax
import jax.numpy as jnp
from jax import lax
from jax.experimental import pallas as pl
from jax.experimental.pallas import tpu as pltpu

D_MODEL = 1024
ATT_HEADS = 8
HEAD_DIM = 64
D_ATT = ATT_HEADS * HEAD_DIM
D_CONV = D_MODEL - D_ATT
CONV_WIDTH = 31
N_EXPERTS = 32
TOP_K = 4
D_FF = 1024
SWIGLU_LIMIT = 7.0
SWIGLU_ALPHA = 1.702
RMS_EPS = 1e-6
LN_EPS = 1e-5
MASK_VALUE = -1e30
LOG2_E = 1.4426950408889634

LANES = 128
SUBLANES = 8
CONV_ROWS = 64
ROW_TILE = 512
ATT_SUB = 512
ATT_Q_TILES = 2
CONV_HALO = 32
GROUP_TILE = 256
SLOT_ROWS = 16
X_WIDTH = D_MODEL + LANES
TILE_SLOT_ROWS = ROW_TILE * TOP_K + N_EXPERTS * SLOT_ROWS
PIECES_PER_TILE = TILE_SLOT_ROWS // SLOT_ROWS
MIN_PIECES = ROW_TILE * TOP_K // SLOT_ROWS
PERM_CHUNK = 512
PERM_CHUNKS = tuple((s, PERM_CHUNK) for s in range(0, ROW_TILE * TOP_K, PERM_CHUNK)) + (
    (ROW_TILE * TOP_K, N_EXPERTS * SLOT_ROWS // 2),)
PERM_OPTIONAL = (ROW_TILE * TOP_K + N_EXPERTS * SLOT_ROWS // 2, N_EXPERTS * SLOT_ROWS // 2)
VMEM_LIMIT = 56 * 1024 * 1024

AUG_Q = HEAD_DIM
AUG_K = HEAD_DIM + 3
ONES_LANE = ATT_HEADS
PART_STRIDE = 16

f32 = jnp.float32
bf16 = jnp.bfloat16


def _split3(x):
    hi = x.astype(bf16)
    r1 = x - hi.astype(f32)
    mid = r1.astype(bf16)
    lo = (r1 - mid.astype(f32)).astype(bf16)
    return hi, mid, lo


def _selection_matrices():
    sel_q = np.zeros((LANES, ATT_HEADS * LANES), np.float32)
    sel_k = np.zeros((ATT_HEADS * LANES, LANES), np.float32)
    for h in range(ATT_HEADS):
        base = h * LANES
        for p in range(3):
            sel_q[PART_STRIDE * p + h, base + AUG_Q + p] = 1.0
            sel_q[ONES_LANE, base + AUG_K + p] = 1.0
            sel_k[base + AUG_K + p, PART_STRIDE * p + h] = -1.0
            sel_k[base + AUG_Q + p, ONES_LANE] = 1.0
    return jnp.asarray(sel_q, bf16), jnp.asarray(sel_k, bf16)


def _inproj_kernel(x_ref, g_ref, wq_ref, wkt_ref, wv_ref, wf_ref, wglu_ref, bf_ref, bglu_ref,
                   selq_ref, selk_ref, tri_ref, wdw_ref, bdw_ref, lng_ref, lnb_ref,
                   qa_ref, kta_ref, v_ref, conv_ref, carry_ref, ext_ref, *, tiles_per_seq):
    i = pl.program_id(0)
    tm = x_ref.shape[0]

    @pl.when(i % tiles_per_seq == 0)
    def _():
        carry_ref[...] = jnp.zeros_like(carry_ref)
        ext_ref[0, 0:CONV_HALO, :] = jnp.zeros((CONV_HALO, D_CONV), f32)

    x = x_ref[...]
    xn = x * lax.rsqrt(jnp.mean(x * x, axis=-1, keepdims=True) + RMS_EPS) * g_ref[...]
    xb = xn.astype(bf16)

    glu = jnp.dot(xb, wglu_ref[...], preferred_element_type=f32) + bglu_ref[...]
    ext_ref[0, CONV_HALO:, :] = glu[:, :D_CONV] * (1.0 / (1.0 + jnp.exp(-glu[:, D_CONV:])))
    span = tm + CONV_HALO - SUBLANES
    for s in range(1, SUBLANES):
        ext_ref[s, 0:span, :] = ext_ref[0, s:s + span, :]
    shift = CONV_HALO - (CONV_WIDTH - 1)
    for cb in range(tm // CONV_ROWS):
        r0 = cb * CONV_ROWS
        acc = jnp.zeros((CONV_ROWS, D_CONV), f32) + bdw_ref[...]
        for j in range(CONV_WIDTH):
            s = (shift + j) % SUBLANES
            a = shift + j - s
            acc = acc + wdw_ref[j:j + 1, :] * ext_ref[s, r0 + a:r0 + a + CONV_ROWS, :]
        mu = jnp.mean(acc, axis=-1, keepdims=True)
        d = acc - mu
        var = jnp.mean(d * d, axis=-1, keepdims=True)
        y = d * lax.rsqrt(var + LN_EPS) * lng_ref[...] + lnb_ref[...]
        conv_ref[r0:r0 + CONV_ROWS, :] = (y * (1.0 / (1.0 + jnp.exp(-y)))).astype(conv_ref.dtype)
    ext_ref[0, 0:CONV_HALO, :] = ext_ref[0, tm:tm + CONV_HALO, :]

    f = jnp.dot(xb, wf_ref[...], preferred_element_type=f32) + bf_ref[...]
    log_f = jnp.minimum(f, 0.0) - jnp.log1p(jnp.exp(-jnp.abs(f)))
    lane = lax.broadcasted_iota(jnp.int32, log_f.shape, 1)
    log_f = jnp.where(((lane & (PART_STRIDE - 1)) < ATT_HEADS) & (lane < 3 * PART_STRIDE), log_f, 0.0)
    tri = tri_ref[...]
    hi, mid, lo = _split3(log_f)
    c = (jnp.dot(tri, hi, preferred_element_type=f32)
         + jnp.dot(tri, mid, preferred_element_type=f32)
         + jnp.dot(tri, lo, preferred_element_type=f32)) + carry_ref[...]
    carry_ref[...] = c[tm - 1:tm, :]
    c = c * LOG2_E

    c_hi = c.astype(bf16).astype(f32)
    c_mid = (c - c_hi).astype(bf16).astype(f32)
    c_lo = c - c_hi - c_mid
    parts = jnp.where(lane < PART_STRIDE, c_hi, jnp.where(lane < 2 * PART_STRIDE, c_mid, c_lo))
    parts = jnp.where(lane == ONES_LANE, 1.0, parts)

    qa = jnp.dot(xb, wq_ref[...], preferred_element_type=f32) * (LOG2_E / math.sqrt(HEAD_DIM))
    qa = (qa + jnp.dot(parts.astype(bf16), selq_ref[...], preferred_element_type=f32)).astype(bf16)
    for h in range(ATT_HEADS):
        qa_ref[0, h] = qa[:, h * LANES:(h + 1) * LANES]

    kta = lax.dot_general(wkt_ref[...], xb, (((1,), (1,)), ((), ())), preferred_element_type=f32)
    kta = (kta + jnp.dot(selk_ref[...], parts.T.astype(bf16), preferred_element_type=f32)).astype(bf16)
    for h in range(ATT_HEADS):
        kta_ref[0, h, 0] = kta[h * LANES:(h + 1) * LANES, :]

    v_ref[...] = jnp.dot(xb, wv_ref[...], preferred_element_type=f32).astype(bf16)


def _inproj(x2d, norm_mix_g, w_in, b_f, b_glu, w_dw, b_dw, ln_g, ln_b, *, batch, seq):
    n = batch * seq
    tm = ROW_TILE
    nt = seq // tm
    wq = w_in[:, :D_ATT]
    wk = w_in[:, D_ATT:2 * D_ATT]
    wv = w_in[:, 2 * D_ATT:3 * D_ATT]
    wf = w_in[:, 3 * D_ATT:3 * D_ATT + ATT_HEADS]
    wglu = w_in[:, 3 * D_ATT + ATT_HEADS:]
    pad = LANES - HEAD_DIM
    wq_pad = jnp.pad(wq.reshape(D_MODEL, ATT_HEADS, HEAD_DIM), ((0, 0), (0, 0), (0, pad)))
    wq_pad = wq_pad.reshape(D_MODEL, ATT_HEADS * LANES).astype(bf16)
    wkt_pad = jnp.pad(wk.T.reshape(ATT_HEADS, HEAD_DIM, D_MODEL), ((0, 0), (0, pad), (0, 0)))
    wkt_pad = wkt_pad.reshape(ATT_HEADS * LANES, D_MODEL).astype(bf16)
    spread = lambda a: jnp.pad(jnp.concatenate(
        [jnp.pad(a, ((0, 0), (0, PART_STRIDE - ATT_HEADS)))] * 3, axis=1),
        ((0, 0), (0, LANES - 3 * PART_STRIDE)))
    wf_pad = spread(wf).astype(bf16)
    bf_pad = spread(b_f.reshape(1, ATT_HEADS))
    sel_q, sel_k = _selection_matrices()
    tri = jnp.asarray(np.tril(np.ones((tm, tm), np.float32)), bf16)

    const = lambda shape: pl.BlockSpec(shape, lambda i: (0,) * len(shape))
    return pl.pallas_call(
        functools.partial(_inproj_kernel, tiles_per_seq=nt),
        grid=(n // tm,),
        in_specs=[
            pl.BlockSpec((tm, D_MODEL), lambda i: (i, 0)),
            const((1, D_MODEL)),
            const((D_MODEL, ATT_HEADS * LANES)),
            const((ATT_HEADS * LANES, D_MODEL)),
            const((D_MODEL, D_ATT)),
            const((D_MODEL, LANES)),
            const((D_MODEL, 2 * D_CONV)),
            const((1, LANES)),
            const((1, 2 * D_CONV)),
            const((LANES, ATT_HEADS * LANES)),
            const((ATT_HEADS * LANES, LANES)),
            const((tm, tm)),
            const((CONV_HALO, D_CONV)), const((1, D_CONV)), const((1, D_CONV)), const((1, D_CONV)),
        ],
        out_specs=[
            pl.BlockSpec((1, ATT_HEADS, tm, LANES), lambda i: (i // nt, 0, i % nt, 0)),
            pl.BlockSpec((1, ATT_HEADS, 1, LANES, tm), lambda i: (i // nt, 0, i % nt, 0, 0)),
            pl.BlockSpec((tm, D_ATT), lambda i: (i, 0)),
            pl.BlockSpec((tm, D_CONV), lambda i: (i, 0)),
        ],
        out_shape=[
            jax.ShapeDtypeStruct((batch, ATT_HEADS, seq, LANES), bf16),
            jax.ShapeDtypeStruct((batch, ATT_HEADS, nt, LANES, tm), bf16),
            jax.ShapeDtypeStruct((n, D_ATT), bf16),
            jax.ShapeDtypeStruct((n, D_CONV), bf16),
        ],
        scratch_shapes=[pltpu.VMEM((1, LANES), f32),
                        pltpu.VMEM((SUBLANES, tm + CONV_HALO, D_CONV), f32)],
        compiler_params=pltpu.CompilerParams(
            dimension_semantics=("arbitrary",), vmem_limit_bytes=VMEM_LIMIT),
        name="inproj",
    )(x2d, norm_mix_g.reshape(1, D_MODEL), wq_pad, wkt_pad, wv.astype(bf16), wf_pad,
      wglu.astype(bf16), bf_pad, b_glu.reshape(1, 2 * D_CONV), sel_q, sel_k, tri,
      jnp.pad(w_dw, ((0, CONV_HALO - CONV_WIDTH), (0, 0))), b_dw.reshape(1, D_CONV),
      ln_g.reshape(1, D_CONV), ln_b.reshape(1, D_CONV))


def _attn_kernel(qa_ref, kta_ref, v_ref, o_ref, acc_ref, m_ref, *, tile):
    i = pl.program_id(2)
    q_rows = ATT_Q_TILES * tile
    first_half = lax.broadcasted_iota(jnp.int32, (tile, LANES), 1) < HEAD_DIM
    acc_ref[...] = jnp.zeros_like(acc_ref)
    m_ref[...] = jnp.full_like(m_ref, MASK_VALUE)
    sub = ATT_SUB
    blocks_per_tile = tile // sub

    def process(kv_tiles):
        v_augs = []
        for j, _ in kv_tiles:
            vp = v_ref[0, pl.ds(pl.multiple_of(j * tile, tile), tile), :]
            one = jnp.ones_like(vp)
            v_augs.append((jnp.where(first_half, vp, one), jnp.where(first_half, one, vp)))
        chains = [(t, h, rb) for t in range(len(kv_tiles)) for h in range(2)
                  for rb in range(q_rows // sub) if kv_tiles[t][1][rb // blocks_per_tile] is not None]

        def visible_width(t, rb):
            if kv_tiles[t][1][rb // blocks_per_tile] == "diag":
                return (rb % blocks_per_tile + 1) * sub
            return tile

        def scores(t, h, rb):
            return jnp.dot(qa_ref[0, h, rb * sub:(rb + 1) * sub, :],
                           kta_ref[0, h, kv_tiles[t][0]][:, :visible_width(t, rb)],
                           preferred_element_type=f32)

        s_next = scores(*chains[0])
        for ci, (t, h, rb) in enumerate(chains):
            s = s_next
            if ci + 1 < len(chains):
                s_next = scores(*chains[ci + 1])
            rows = slice(rb * sub, (rb + 1) * sub)
            width = s.shape[1]
            if kv_tiles[t][1][rb // blocks_per_tile] == "diag":
                r = lax.broadcasted_iota(jnp.int32, s.shape, 0) + (rb % blocks_per_tile) * sub
                c = lax.broadcasted_iota(jnp.int32, s.shape, 1)
                s = jnp.where(c <= r, s, MASK_VALUE)
            m_old = m_ref[h, rows, :]
            m_new = jnp.maximum(m_old, jnp.max(s, axis=-1, keepdims=True))
            alpha = jnp.exp2(m_old - m_new)
            p = jnp.exp2(s - jnp.concatenate([m_new] * (width // LANES), axis=1)).astype(bf16)
            acc_ref[h, rows, :] = alpha * acc_ref[h, rows, :] + jnp.dot(
                p, v_augs[t][h][:width, :], preferred_element_type=f32)
            m_ref[h, rows, :] = m_new

    everything = ("full",) * ATT_Q_TILES

    def body(jj, carry):
        process([(ATT_Q_TILES * jj + t, everything) for t in range(ATT_Q_TILES)])
        return carry

    lax.fori_loop(0, i, body, 0)
    process([(ATT_Q_TILES * i + t,
              tuple("full" if t < qt else ("diag" if t == qt else None) for qt in range(ATT_Q_TILES)))
             for t in range(ATT_Q_TILES)])

    for qt in range(ATT_Q_TILES):
        a0 = acc_ref[0, qt * tile:(qt + 1) * tile, :]
        a1 = acc_ref[1, qt * tile:(qt + 1) * tile, :]
        l0 = a0[:, HEAD_DIM:HEAD_DIM + 1]
        l1 = a1[:, 0:1]
        o_ref[0, qt * tile:(qt + 1) * tile, :] = jnp.where(
            first_half, a0 / l0, a1 / l1).astype(o_ref.dtype)


def _attention(qa, kta, v3, *, batch, seq):
    tile = ROW_TILE
    nt = seq // tile
    q_rows = ATT_Q_TILES * tile
    return pl.pallas_call(
        functools.partial(_attn_kernel, tile=tile),
        grid=(batch, ATT_HEADS // 2, seq // q_rows),
        in_specs=[
            pl.BlockSpec((1, 2, q_rows, LANES), lambda b, hp, i: (b, hp, i, 0)),
            pl.BlockSpec((1, 2, nt, LANES, tile), lambda b, hp, i: (b, hp, 0, 0, 0)),
            pl.BlockSpec((1, seq, LANES), lambda b, hp, i: (b, 0, hp)),
        ],
        out_specs=pl.BlockSpec((1, q_rows, LANES), lambda b, hp, i: (b, i, hp)),
        out_shape=jax.ShapeDtypeStruct((batch, seq, D_ATT), bf16),
        scratch_shapes=[pltpu.VMEM((2, q_rows, LANES), f32), pltpu.VMEM((2, q_rows, LANES), f32)],
        compiler_params=pltpu.CompilerParams(
            dimension_semantics=("arbitrary", "arbitrary", "arbitrary"),
            vmem_limit_bytes=VMEM_LIMIT),
        name="attention",
    )(qa, kta, v3)


def _outproj_router_kernel(att_ref, conv_ref, x_ref, wo_ref, g_ref, wr_hi_ref, wr_lo_ref, br_ref,
                           triu_ref, tril_ref, h1_ref, hna_ref, dl_ref, col_ref, cnt_ref):
    h1 = (x_ref[...]
          + jnp.dot(att_ref[...], wo_ref[:D_ATT, :], preferred_element_type=f32)
          + jnp.dot(conv_ref[...], wo_ref[D_ATT:, :], preferred_element_type=f32))
    h1_ref[...] = h1
    hn = h1 * lax.rsqrt(jnp.mean(h1 * h1, axis=-1, keepdims=True) + RMS_EPS) * g_ref[...]

    hn_hi = hn.astype(bf16)
    hn_lo = (hn - hn_hi.astype(f32)).astype(bf16)
    nt = (((1,), (1,)), ((), ()))
    logits = (lax.dot_general(wr_hi_ref[...], hn_hi, nt, preferred_element_type=f32)
              + lax.dot_general(wr_hi_ref[...], hn_lo, nt, preferred_element_type=f32)
              + lax.dot_general(wr_lo_ref[...], hn_hi, nt, preferred_element_type=f32)
              + br_ref[...])

    erow = lax.broadcasted_iota(jnp.int32, logits.shape, 0)
    work = logits
    vals, idxs = [], []
    for _ in range(TOP_K):
        mk = jnp.max(work, axis=0, keepdims=True)
        ik = jnp.min(jnp.where(work == mk, erow, N_EXPERTS), axis=0, keepdims=True)
        work = jnp.where(erow == ik, -jnp.inf, work)
        vals.append(mk)
        idxs.append(ik)
    exps = [jnp.exp(v - vals[0]) for v in vals]
    denom = exps[0] + exps[1] + exps[2] + exps[3]

    onehot = jnp.zeros(logits.shape, f32)
    gates = jnp.zeros(logits.shape, f32)
    for k in range(TOP_K):
        hit = erow == idxs[k]
        onehot = onehot + jnp.where(hit, 1.0, 0.0)
        gates = gates + jnp.where(hit, exps[k] / denom, 0.0)

    before = jnp.dot(onehot.astype(bf16), triu_ref[...], preferred_element_type=f32)
    count = jnp.sum(onehot, axis=1, keepdims=True)
    slots = jnp.floor((count + (SLOT_ROWS - 1)) * (1.0 / SLOT_ROWS))
    slots_b = jnp.broadcast_to(slots, (N_EXPERTS, LANES))
    slot_start = jnp.dot(tril_ref[...], slots_b.astype(bf16), preferred_element_type=f32)
    cnt_ref[0] = slots_b
    pos = before + slot_start[:, 0:1] * SLOT_ROWS
    dls = []
    for k in range(TOP_K):
        dk = jnp.sum(jnp.where(erow == idxs[k], pos, 0.0), axis=0, keepdims=True)
        dl_ref[k:k + 1, :] = dk.astype(jnp.int32)
        dls.append(dk)

    g_hi, g_mid, g_lo = _split3(gates)
    tm = logits.shape[1]
    r8 = lax.broadcasted_iota(jnp.int32, (8, tm), 0)
    dl8 = jnp.zeros((8, tm), f32)
    for k in range(TOP_K):
        dl8 = jnp.where(r8 == k, dls[k], dl8)
    stack = jnp.concatenate(
        [g_hi.astype(f32), g_mid.astype(f32), g_lo.astype(f32), dl8,
         jnp.zeros((LANES - 3 * N_EXPERTS - 8, tm), f32)], axis=0)
    col = stack.T
    col_ref[...] = col
    lane = lax.broadcasted_iota(jnp.int32, col.shape, 1)
    hna_ref[:, :D_MODEL] = hn_hi
    hna_ref[:, D_MODEL:] = jnp.where(lane < 3 * N_EXPERTS, col, 0.0).astype(bf16)


def _outproj_router(att2d, conv2d, x2d, w_out, norm_ffn_g, w_router, b_router):
    n = x2d.shape[0]
    tm = ROW_TILE
    wr_t = w_router.T
    wr_hi = wr_t.astype(bf16)
    wr_lo = (wr_t - wr_hi.astype(f32)).astype(bf16)
    triu = jnp.asarray(np.triu(np.ones((tm, tm), np.float32), k=1), bf16)
    tril = jnp.asarray(np.tril(np.ones((N_EXPERTS, N_EXPERTS), np.float32), k=-1), bf16)
    const = lambda shape: pl.BlockSpec(shape, lambda i: (0,) * len(shape))
    rows = lambda width: pl.BlockSpec((tm, width), lambda i: (i, 0))
    return pl.pallas_call(
        _outproj_router_kernel,
        grid=(n // tm,),
        in_specs=[
            rows(D_ATT), rows(D_CONV), rows(D_MODEL),
            const((D_MODEL, D_MODEL)), const((1, D_MODEL)),
            const((N_EXPERTS, D_MODEL)), const((N_EXPERTS, D_MODEL)), const((N_EXPERTS, 1)),
            const((tm, tm)), const((N_EXPERTS, N_EXPERTS)),
        ],
        out_specs=[
            rows(D_MODEL), rows(X_WIDTH),
            pl.BlockSpec((TOP_K, tm), lambda i: (0, i)),
            rows(LANES),
            pl.BlockSpec((1, N_EXPERTS, LANES), lambda i: (i, 0, 0)),
        ],
        out_shape=[
            jax.ShapeDtypeStruct((n, D_MODEL), f32),
            jax.ShapeDtypeStruct((n, X_WIDTH), bf16),
            jax.ShapeDtypeStruct((TOP_K, n), jnp.int32),
            jax.ShapeDtypeStruct((n, LANES), f32),
            jax.ShapeDtypeStruct((n // tm, N_EXPERTS, LANES), f32),
        ],
        compiler_params=pltpu.CompilerParams(
            dimension_semantics=("arbitrary",), vmem_limit_bytes=VMEM_LIMIT),
        name="outproj_router",
    )(att2d, conv2d, x2d, w_out.astype(bf16), norm_ffn_g.reshape(1, D_MODEL), wr_hi, wr_lo,
      b_router.reshape(N_EXPERTS, 1), triu, tril)


def _dispatch_kernel(pdest_ref, npiece_ref, padd_ref, npad_ref, nused_ref, dl_ref, hna_ref, xs_ref,
                     xbuf, zbuf, sems, zsem, *, n_tiles, n_group_tiles):
    i = pl.program_id(0)
    slot = i % 2

    def piece_copy(t, s, j):
        dst = pl.multiple_of(pdest_ref[t * PIECES_PER_TILE + j], SLOT_ROWS)
        return pltpu.make_async_copy(
            xbuf.at[s, pl.ds(pl.multiple_of(j * SLOT_ROWS, SLOT_ROWS), SLOT_ROWS)],
            xs_ref.at[pl.ds(dst, SLOT_ROWS)], sems.at[s])

    def for_each_piece(t, s, action):
        for j in range(MIN_PIECES):
            action(piece_copy(t, s, j), j % 2)

        def rest(j, c):
            action(piece_copy(t, s, j), 0)
            return c
        lax.fori_loop(MIN_PIECES, npiece_ref[t], rest, 0)

    def pad_copy(m):
        dst = pl.multiple_of(padd_ref[m], SLOT_ROWS)
        return pltpu.make_async_copy(
            zbuf.at[pl.ds(0, SLOT_ROWS)], xs_ref.at[pl.ds(dst, SLOT_ROWS)], zsem)

    def tail_copy(t):
        dst = pl.multiple_of(t * GROUP_TILE, GROUP_TILE)
        return pltpu.make_async_copy(zbuf, xs_ref.at[pl.ds(dst, GROUP_TILE)], zsem)

    @pl.when(i >= 2)
    def _():
        for_each_piece(jnp.maximum(i - 2, 0), slot, lambda c, lane: c.wait())

    @pl.when(i < n_tiles)
    def _():
        tile_idx = jnp.minimum(i, n_tiles - 1)

        def permute_rows(start, size):
            dl = dl_ref[...]
            r = lax.broadcasted_iota(jnp.int32, (size, dl.shape[1]), 0) + start
            p = jnp.zeros(r.shape, f32)
            for k in range(TOP_K):
                p = jnp.where(dl[k:k + 1, :] == r, 1.0, p)
            xbuf[slot, start:start + size, :] = jnp.dot(
                p.astype(bf16), hna_ref[...], preferred_element_type=f32).astype(bf16)

        for start, size in PERM_CHUNKS:
            permute_rows(start, size)

        @pl.when(npiece_ref[tile_idx] * SLOT_ROWS > PERM_OPTIONAL[0])
        def _():
            permute_rows(*PERM_OPTIONAL)

        for_each_piece(tile_idx, slot, lambda c, lane: c.start(priority=lane))

    @pl.when(i == n_tiles)
    def _():
        zbuf[...] = jnp.zeros_like(zbuf)

        def start(m, c):
            pad_copy(m).start()
            return c
        lax.fori_loop(0, npad_ref[0], start, 0)

        def start_tail(t, c):
            tail_copy(t).start()
            return c
        lax.fori_loop(nused_ref[0], n_group_tiles, start_tail, 0)

    @pl.when(i == n_tiles + 1)
    def _():
        def wait(m, c):
            pad_copy(m).wait()
            return c
        lax.fori_loop(0, npad_ref[0], wait, 0)

        def wait_tail(t, c):
            tail_copy(t).wait()
            return c
        lax.fori_loop(nused_ref[0], n_group_tiles, wait_tail, 0)


def _dispatch(hna, dl, plan, n_rows):
    n = hna.shape[0]
    tm = ROW_TILE
    nt = n // tm
    last = nt - 1
    grid_spec = pltpu.PrefetchScalarGridSpec(
        num_scalar_prefetch=5,
        grid=(nt + 2,),
        in_specs=[
            pl.BlockSpec((TOP_K, tm), lambda i, *_: (0, jnp.minimum(i, last))),
            pl.BlockSpec((tm, X_WIDTH), lambda i, *_: (jnp.minimum(i, last), 0)),
        ],
        out_specs=pl.BlockSpec(memory_space=pl.ANY),
        scratch_shapes=[
            pltpu.VMEM((2, TILE_SLOT_ROWS, X_WIDTH), bf16),
            pltpu.VMEM((GROUP_TILE, X_WIDTH), bf16),
            pltpu.SemaphoreType.DMA((2,)),
            pltpu.SemaphoreType.DMA,
        ],
    )
    return pl.pallas_call(
        functools.partial(_dispatch_kernel, n_tiles=nt, n_group_tiles=n_rows // GROUP_TILE),
        grid_spec=grid_spec,
        out_shape=jax.ShapeDtypeStruct((n_rows, X_WIDTH), bf16),
        compiler_params=pltpu.CompilerParams(
            dimension_semantics=("arbitrary",), vmem_limit_bytes=VMEM_LIMIT),
        name="dispatch",
    )(plan["piece_dest"], plan["n_pieces"], plan["pad_dest"], plan["n_pad"], plan["n_used"], dl, hna)


def _moe_kernel(rstart_ref, rtiles_ref, nused_ref, xs_ref, wgu_hbm, bgu_ref, wd_hbm, bd_ref, ys_ref,
                wgu_f32, wd_f32, wgu_bf, wd_bf, xbuf, ybuf, pend, wsem, xsem, ysem, *, n_group_tiles):
    e = pl.program_id(0)
    n_t = rtiles_ref[e]
    row0 = rstart_ref[e]
    ws = e % 2

    def w_copies(ex, slot):
        return (pltpu.make_async_copy(wgu_hbm.at[ex], wgu_f32.at[slot], wsem.at[slot]),
                pltpu.make_async_copy(wd_hbm.at[ex], wd_f32.at[slot], wsem.at[slot]))

    def x_copy(t, s):
        src = pl.multiple_of(row0 + t * GROUP_TILE, GROUP_TILE)
        return pltpu.make_async_copy(xs_ref.at[pl.ds(src, GROUP_TILE)], xbuf.at[s], xsem.at[s])

    def y_copy(t, s):
        dst = pl.multiple_of(row0 + t * GROUP_TILE, GROUP_TILE)
        return pltpu.make_async_copy(ybuf.at[s], ys_ref.at[pl.ds(dst, GROUP_TILE)], ysem.at[s])

    def tail_copy(t):
        dst = pl.multiple_of(t * GROUP_TILE, GROUP_TILE)
        return pltpu.make_async_copy(ybuf.at[0], ys_ref.at[pl.ds(dst, GROUP_TILE)], ysem.at[0])

    @pl.when(e == 0)
    def _():
        pend[0] = 0
        pend[1] = 0
        for c in w_copies(0, 0):
            c.start()

    @pl.when(n_t > 0)
    def _():
        x_copy(0, 0).start()

    @pl.when(n_t > 1)
    def _():
        x_copy(1, 1).start()

    @pl.when(e + 1 < N_EXPERTS)
    def _():
        for c in w_copies(jnp.minimum(e + 1, N_EXPERTS - 1), 1 - ws):
            c.start()

    for c in w_copies(e, ws):
        c.wait()

    @pl.when(n_t > 0)
    def _():
        wgu_bf[...] = wgu_f32[ws].astype(bf16)
        wd_bf[...] = wd_f32[ws].astype(bf16)

        def body(t, carry):
            s = t % 2
            x_copy(t, s).wait()

            @pl.when((t >= 2) | (pend[s] == 1))
            def _():
                y_copy(t, s).wait()
                pend[s] = 0

            h = jnp.dot(xbuf[s, :, :D_MODEL], wgu_bf[...], preferred_element_type=f32) + bgu_ref[0]
            hg = jnp.minimum(h[:, :D_FF], SWIGLU_LIMIT)
            hu = jnp.clip(h[:, D_FF:], -SWIGLU_LIMIT, SWIGLU_LIMIT)
            act = (hu + 1.0) * (hg * (1.0 / (1.0 + jnp.exp(-SWIGLU_ALPHA * hg))))
            y = jnp.dot(act.astype(bf16), wd_bf[...], preferred_element_type=f32) + bd_ref[0]
            aug = xbuf[s, :, D_MODEL:].astype(f32)
            lane = lax.broadcasted_iota(jnp.int32, aug.shape, 1)
            gate = jnp.sum(jnp.where((lane & (N_EXPERTS - 1)) == e, aug, 0.0), axis=1, keepdims=True)
            ybuf[s] = (gate * y).astype(ybuf.dtype)
            y_copy(t, s).start()

            @pl.when(t + 2 < n_t)
            def _():
                x_copy(t + 2, s).start()

            return carry

        lax.fori_loop(0, n_t, body, 0)
        pend[(n_t - 1) % 2] = 1

        @pl.when(n_t >= 2)
        def _():
            pend[n_t % 2] = 1

    @pl.when(e == N_EXPERTS - 1)
    def _():
        for s in range(2):
            @pl.when(pend[s] == 1)
            def _():
                y_copy(0, s).wait()
                pend[s] = 0

        ybuf[0] = jnp.zeros(ybuf.shape[1:], ybuf.dtype)

        def start_tail(t, carry):
            tail_copy(t).start()
            return carry

        def wait_tail(t, carry):
            tail_copy(t).wait()
            return carry

        lax.fori_loop(nused_ref[0], n_group_tiles, start_tail, 0)
        lax.fori_loop(nused_ref[0], n_group_tiles, wait_tail, 0)


def _moe(xs, plan, w_gu, b_gu, w_down, b_down):
    n_rows = xs.shape[0]
    tg = GROUP_TILE
    grid_spec = pltpu.PrefetchScalarGridSpec(
        num_scalar_prefetch=3,
        grid=(N_EXPERTS,),
        in_specs=[
            pl.BlockSpec(memory_space=pl.ANY),
            pl.BlockSpec(memory_space=pl.ANY),
            pl.BlockSpec((1, 1, 2 * D_FF), lambda e, *_: (e, 0, 0)),
            pl.BlockSpec(memory_space=pl.ANY),
            pl.BlockSpec((1, 1, D_MODEL), lambda e, *_: (e, 0, 0)),
        ],
        out_specs=pl.BlockSpec(memory_space=pl.ANY),
        scratch_shapes=[
            pltpu.VMEM((2, D_MODEL, 2 * D_FF), f32),
            pltpu.VMEM((2, D_FF, D_MODEL), f32),
            pltpu.VMEM((D_MODEL, 2 * D_FF), bf16),
            pltpu.VMEM((D_FF, D_MODEL), bf16),
            pltpu.VMEM((2, tg, X_WIDTH), bf16),
            pltpu.VMEM((2, tg, D_MODEL), bf16),
            pltpu.SMEM((2,), jnp.int32),
            pltpu.SemaphoreType.DMA((2,)),
            pltpu.SemaphoreType.DMA((2,)),
            pltpu.SemaphoreType.DMA((2,)),
        ],
    )
    return pl.pallas_call(
        functools.partial(_moe_kernel, n_group_tiles=n_rows // tg),
        grid_spec=grid_spec,
        out_shape=jax.ShapeDtypeStruct((n_rows, D_MODEL), bf16),
        compiler_params=pltpu.CompilerParams(
            dimension_semantics=("arbitrary",), vmem_limit_bytes=VMEM_LIMIT),
        name="moe_experts",
    )(plan["region_start"], plan["region_tiles"], plan["n_used"], xs, w_gu,
      b_gu.reshape(N_EXPERTS, 1, 2 * D_FF), w_down, b_down.reshape(N_EXPERTS, 1, D_MODEL))


def _combine_kernel(pdest_ref, npiece_ref, h1_ref, col_ref, g_ref, ys_ref, o_ref, ybuf, sems,
                    *, n_tiles):
    i = pl.program_id(0)
    slot = i % 2

    def piece_copy(t, s, j):
        src = pl.multiple_of(pdest_ref[t * PIECES_PER_TILE + j], SLOT_ROWS)
        return pltpu.make_async_copy(
            ys_ref.at[pl.ds(src, SLOT_ROWS)],
            ybuf.at[s, pl.ds(pl.multiple_of(j * SLOT_ROWS, SLOT_ROWS), SLOT_ROWS)], sems.at[s])

    def for_each_piece(t, s, action):
        for j in range(MIN_PIECES):
            action(piece_copy(t, s, j), j % 2)

        def rest(j, c):
            action(piece_copy(t, s, j), 0)
            return c
        lax.fori_loop(MIN_PIECES, npiece_ref[t], rest, 0)

    def fetch(t, s):
        for_each_piece(t, s, lambda c, lane: c.start(priority=lane))

    @pl.when(i == 0)
    def _():
        ybuf[...] = jnp.zeros_like(ybuf)
        fetch(0, 0)

    @pl.when(i + 1 < n_tiles)
    def _():
        fetch(jnp.minimum(i + 1, n_tiles - 1), 1 - slot)

    for_each_piece(i, slot, lambda c, lane: c.wait())

    col = col_ref[...]
    rows = [col[:, 3 * N_EXPERTS + k:3 * N_EXPERTS + k + 1].astype(jnp.int32) for k in range(TOP_K)]

    def gathered(start, size):
        r = lax.broadcasted_iota(jnp.int32, (col.shape[0], size), 1) + start
        g = jnp.zeros(r.shape, f32)
        for k in range(TOP_K):
            g = jnp.where(rows[k] == r, 1.0, g)
        return jnp.dot(g.astype(bf16), ybuf[slot, start:start + size, :], preferred_element_type=f32)

    h = h1_ref[...]
    for start, size in PERM_CHUNKS:
        h = h + gathered(start, size)
    o_ref[...] = h

    @pl.when(npiece_ref[i] * SLOT_ROWS > PERM_OPTIONAL[0])
    def _():
        o_ref[...] += gathered(*PERM_OPTIONAL)

    h = o_ref[...]
    o_ref[...] = h * lax.rsqrt(jnp.mean(h * h, axis=-1, keepdims=True) + RMS_EPS) * g_ref[...]


def _combine(h1, ys, col, plan, norm_final_g):
    n = h1.shape[0]
    tm = ROW_TILE
    nt = n // tm
    grid_spec = pltpu.PrefetchScalarGridSpec(
        num_scalar_prefetch=2,
        grid=(nt,),
        in_specs=[
            pl.BlockSpec((tm, D_MODEL), lambda i, *_: (i, 0)),
            pl.BlockSpec((tm, LANES), lambda i, *_: (i, 0)),
            pl.BlockSpec((1, D_MODEL), lambda i, *_: (0, 0)),
            pl.BlockSpec(memory_space=pl.ANY),
        ],
        out_specs=pl.BlockSpec((tm, D_MODEL), lambda i, *_: (i, 0)),
        scratch_shapes=[
            pltpu.VMEM((2, TILE_SLOT_ROWS, D_MODEL), bf16),
            pltpu.SemaphoreType.DMA((2,)),
        ],
    )
    return pl.pallas_call(
        functools.partial(_combine_kernel, n_tiles=nt),
        grid_spec=grid_spec,
        out_shape=jax.ShapeDtypeStruct((n, D_MODEL), f32),
        compiler_params=pltpu.CompilerParams(
            dimension_semantics=("arbitrary",), vmem_limit_bytes=VMEM_LIMIT),
        name="combine",
    )(plan["piece_dest"], plan["n_pieces"], h1, col, norm_final_g.reshape(1, D_MODEL), ys)


def _routing_plan(slot_counts):
    tg = GROUP_TILE
    i32 = jnp.int32
    pc = slot_counts.astype(i32) * SLOT_ROWS
    local_end = jnp.cumsum(pc, axis=1)
    local_start = local_end - pc
    total = jnp.sum(pc, axis=0)
    region = ((total + tg - 1) // tg) * tg
    region_end = jnp.cumsum(region)
    region_start = region_end - region
    base = region_start[None, :] + jnp.cumsum(pc, axis=0) - pc
    n_pieces = local_end[:, -1] // SLOT_ROWS

    piece_row = jnp.arange(PIECES_PER_TILE, dtype=i32) * SLOT_ROWS
    owner = jnp.sum((local_end[:, None, :] <= piece_row[None, :, None]).astype(i32), axis=2)
    owner_hit = owner[:, :, None] == jnp.arange(N_EXPERTS, dtype=i32)[None, None, :]
    shift = jnp.sum(jnp.where(owner_hit, (base - local_start)[:, None, :], 0), axis=2)
    piece_dest = shift + piece_row[None, :]
    piece_dest = jnp.where(piece_row[None, :] < local_end[:, -1:], piece_dest, 0)

    pad_slots = (region - total) // SLOT_ROWS
    pad_end = jnp.cumsum(pad_slots)
    m = jnp.arange(N_EXPERTS * (tg // SLOT_ROWS), dtype=i32)
    pad_owner = jnp.minimum(jnp.sum((pad_end[None, :] <= m[:, None]).astype(i32), axis=1),
                            N_EXPERTS - 1)
    pad_hit = pad_owner[:, None] == jnp.arange(N_EXPERTS, dtype=i32)[None, :]
    pad_first = jnp.sum(jnp.where(pad_hit, (region_start + total - (pad_end - pad_slots) * SLOT_ROWS)
                                  [None, :], 0), axis=1)
    pad_dest = jnp.where(m < pad_end[-1], pad_first + m * SLOT_ROWS, 0)

    return {
        "piece_dest": piece_dest.reshape(-1), "n_pieces": n_pieces, "pad_dest": pad_dest,
        "n_pad": pad_end[-1:],
        "region_start": region_start, "region_tiles": region // tg,
        "n_used": (region_end[-1] // tg).reshape(1),
    }


def kernel(x, norm_mix_g, w_in, b_f, b_glu, w_dw, b_dw, ln_g, ln_b, w_out, norm_ffn_g, w_router,
           b_router, w_gu, b_gu, w_down, b_down, norm_final_g):
    batch, seq, d = x.shape
    assert d == D_MODEL and seq % (ATT_Q_TILES * ROW_TILE) == 0, (batch, seq, d)
    n = batch * seq
    x2d = x.reshape(n, d)

    qa, kta, v, conv = _inproj(x2d, norm_mix_g, w_in, b_f, b_glu, w_dw, b_dw, ln_g, ln_b,
                               batch=batch, seq=seq)
    att = _attention(qa, kta, v.reshape(batch, seq, D_ATT), batch=batch, seq=seq)

    h1, hna, dl, col, slot_counts = _outproj_router(
        att.reshape(n, D_ATT), conv, x2d, w_out, norm_ffn_g, w_router, b_router)

    n_token_tiles = n // ROW_TILE
    max_rows = n * TOP_K + N_EXPERTS * (n_token_tiles * (SLOT_ROWS - 1) + GROUP_TILE - 1)
    n_group_tiles = -(-max_rows // GROUP_TILE)
    plan = _routing_plan(slot_counts[:, :, 0])
    xs = _dispatch(hna, dl, plan, n_group_tiles * GROUP_TILE)
    ys = _moe(xs, plan, w_gu, b_gu, w_down, b_down)
    out = _combine(h1, ys, col, plan, norm_final_g)
    return out.reshape(batch, seq, d)
```
